```python
import math
import jax, jax.numpy as jnp
from jax import lax
import numpy as np

D_MODEL = 1024
BATCH = 2
SEQ = 16384
DEPTH = 2

N_HEADS = 8
N_KV_HEADS = 2
HEAD_DIM = 64
GQA_GROUP = N_HEADS // N_KV_HEADS
WINDOW = 128
ATTN_BLOCK = WINDOW
Q_DIM = N_HEADS * HEAD_DIM
KV_DIM = N_KV_HEADS * HEAD_DIM
N_BUCKETS = 32
MAX_DISTANCE = 128
POOL_WINDOWS = (2, 4, 8, 16)
POOL_GROUPS = 4
POOL_GROUP_DIM = 128
POOL_DIM = POOL_GROUPS * POOL_GROUP_DIM
GMLP_CHUNK = 128
GMLP_GROUPS = 4
GMLP_DIM = 512
GMLP_GROUP_DIM = GMLP_DIM // GMLP_GROUPS
N_BRANCHES = 3
GATE_DIM = N_BRANCHES * D_MODEL
SPLIT_POINTS = (Q_DIM, Q_DIM + KV_DIM, Q_DIM + 2 * KV_DIM, Q_DIM + 2 * KV_DIM + POOL_DIM, Q_DIM + 2 * KV_DIM + POOL_DIM + GMLP_DIM, Q_DIM + 2 * KV_DIM + POOL_DIM + 2 * GMLP_DIM)
IN_DIM = Q_DIM + 2 * KV_DIM + POOL_DIM + 2 * GMLP_DIM + GATE_DIM
PEER_HEADS = 8
N_KEYS = 128
N_EXPERTS = N_KEYS * N_KEYS
PEER_TOPK = 16
PEER_QDIM = 256
PEER_HALF = PEER_QDIM // 2
PEER_BLOCK = 128
EPS = 1e-6
NEG_INF = -1e30

kernel_name = "hybrid_gated_swa_pool_gmlp_peer"


def rmsnorm(x, gain):
    xf = x.astype(jnp.float32)
    y = xf * lax.rsqrt(jnp.mean(xf * xf, axis=-1, keepdims=True) + EPS)
    return (y * gain.astype(jnp.float32)).astype(x.dtype)


def t5_causal_bucket(dist):
    n = jnp.maximum(dist, 0)
    max_exact = N_BUCKETS // 2
    nf = jnp.maximum(n, 1).astype(jnp.float32)
    large = max_exact + (jnp.log(nf / max_exact) / math.log(MAX_DISTANCE / max_exact) * (N_BUCKETS - max_exact)).astype(jnp.int32)
    large = jnp.minimum(large, N_BUCKETS - 1)
    return jnp.where(n < max_exact, n, large)


def sliding_window_attention(q, k, v, sinks, rel_bias):
    B, S, _ = q.shape
    nb = S // ATTN_BLOCK
    q = q.reshape(B, nb, ATTN_BLOCK, N_KV_HEADS, GQA_GROUP, HEAD_DIM)
    k = k.reshape(B, nb, ATTN_BLOCK, N_KV_HEADS, HEAD_DIM)
    v = v.reshape(B, nb, ATTN_BLOCK, N_KV_HEADS, HEAD_DIM)

    def with_prev(t):
        prev = jnp.pad(t, ((0, 0), (1, 0), (0, 0), (0, 0), (0, 0)))[:, :-1]
        return jnp.concatenate([prev, t], axis=2)

    kk, vv = with_prev(k), with_prev(v)
    logits = jnp.einsum('bnqhgd,bnshd->bnhgqs', q, kk).astype(jnp.float32) * (HEAD_DIM ** -0.5)
    qi = jnp.arange(ATTN_BLOCK)[:, None]
    kj = jnp.arange(2 * ATTN_BLOCK)[None, :]
    dist = qi + ATTN_BLOCK - kj
    band = (dist >= 0) & (dist < WINDOW)
    bias = rel_bias[t5_causal_bucket(dist)].astype(jnp.float32)
    bias = bias.transpose(2, 0, 1).reshape(N_KV_HEADS, GQA_GROUP, ATTN_BLOCK, 2 * ATTN_BLOCK)
    before_start = (jnp.arange(nb) == 0)[:, None, None] & (kj < ATTN_BLOCK)[None]
    mask = band[None] & ~before_start
    logits = jnp.where(mask[None, :, None, None], logits + bias, NEG_INF)
    sink = sinks.astype(jnp.float32).reshape(N_KV_HEADS, GQA_GROUP)[:, :, None, None]
    m = jnp.maximum(jnp.max(logits, axis=-1, keepdims=True), sink)
    p = jnp.exp(logits - m)
    p = p / (jnp.sum(p, axis=-1, keepdims=True) + jnp.exp(sink - m))
    out = jnp.einsum('bnhgqs,bnshd->bnqhgd', p.astype(v.dtype), vv)
    return out.reshape(B, S, Q_DIM)


def multiscale_pool(z, w_pool, pool_scale):
    B, S, _ = z.shape
    zg = z.astype(jnp.float32).reshape(B, S, POOL_GROUPS, POOL_GROUP_DIM)
    csum = jnp.cumsum(zg, axis=1)
    count = jnp.arange(1, S + 1, dtype=jnp.float32)
    outs = []
    for gi, w in enumerate(POOL_WINDOWS):
        c = csum[:, :, gi]
        lagged = jnp.pad(c, ((0, 0), (w, 0), (0, 0)))[:, :S]
        mean = (c - lagged) / jnp.minimum(count, float(w))[None, :, None]
        outs.append(mean - zg[:, :, gi])
    d = jnp.stack(outs, axis=2).astype(z.dtype)
    y = jnp.einsum('bsgc,gcd->bsgd', d, w_pool)
    return y.reshape(B, S, POOL_DIM) * pool_scale


def chunked_spatial_gating(u, v, v_gain, w_s, b_s):
    B, S, _ = u.shape
    u = jax.nn.gelu(u)
    v = rmsnorm(jax.nn.gelu(v), v_gain)
    nc = S // GMLP_CHUNK
    v = v.reshape(B, nc, GMLP_CHUNK, GMLP_GROUPS, GMLP_GROUP_DIM)
    causal = jnp.tril(jnp.ones((GMLP_CHUNK, GMLP_CHUNK), dtype=bool))
    w = jnp.where(causal[None], w_s, jnp.zeros_like(w_s))
    mixed = jnp.einsum('gts,bcsgd->bctgd', w, v) + b_s.T[:, :, None]
    return u * mixed.reshape(B, S, GMLP_DIM)


def peer_ffn(h, w_query, sub_keys1, sub_keys2, expert_u, expert_v):
    B, S, D = h.shape
    tokens = h.reshape(-1, PEER_BLOCK, D)

    def block(xb):
        q = (xb @ w_query).reshape(PEER_BLOCK, PEER_HEADS, 2, PEER_HALF)
        s1 = jnp.einsum('thd,kd->thk', q[:, :, 0], sub_keys1).astype(jnp.float32)
        s2 = jnp.einsum('thd,kd->thk', q[:, :, 1], sub_keys2).astype(jnp.float32)
        v1, i1 = lax.top_k(s1, PEER_TOPK)
        v2, i2 = lax.top_k(s2, PEER_TOPK)
        cand = (v1[..., :, None] + v2[..., None, :]).reshape(PEER_BLOCK, PEER_HEADS, PEER_TOPK * PEER_TOPK)
        top, ci = lax.top_k(cand, PEER_TOPK)
        idx = jnp.take_along_axis(i1, ci // PEER_TOPK, axis=-1) * N_KEYS + jnp.take_along_axis(i2, ci % PEER_TOPK, axis=-1)
        gate = jax.nn.softmax(top, axis=-1)
        idx = idx.reshape(PEER_BLOCK, PEER_HEADS * PEER_TOPK)
        gate = gate.reshape(PEER_BLOCK, PEER_HEADS * PEER_TOPK)
        u_sel = expert_u[idx]
        act = jax.nn.gelu(jnp.einsum('td,tkd->tk', xb, u_sel).astype(jnp.float32))
        coeff = (gate * act).astype(xb.dtype)
        return jnp.einsum('tk,tkd->td', coeff, expert_v[idx])

    return lax.map(block, tokens).reshape(B, S, D)


def setup_inputs(seed: int = 0) -> dict:
    key = jax.random.key(seed)
    ks = jax.random.split(key, 24)
    f32 = jnp.float32
    L, D = DEPTH, D_MODEL

    def nrm(k, shape, scale):
        return jax.random.normal(k, shape, f32) * scale

    return {
        "x": nrm(ks[0], (BATCH, SEQ, D), 1.0),
        "rel_bias": nrm(ks[1], (N_BUCKETS, N_HEADS), 0.5),
        "final_gain": 1.0 + nrm(ks[2], (D,), 0.02),
        "attn_norm_gain": 1.0 + nrm(ks[3], (L, D), 0.02),
        "w_in": nrm(ks[4], (L, D, IN_DIM), D ** -0.5),
        "b_gate": nrm(ks[5], (L, GATE_DIM), 0.02),
        "attn_sinks": nrm(ks[6], (L, N_HEADS), 0.5),
        "w_pool": nrm(ks[7], (L, POOL_GROUPS, POOL_GROUP_DIM, POOL_GROUP_DIM), POOL_GROUP_DIM ** -0.5),
        "pool_scale": 1.0 + nrm(ks[8], (L, POOL_DIM), 0.02),
        "gmlp_v_gain": 1.0 + nrm(ks[9], (L, GMLP_DIM), 0.02),
        "w_spatial": nrm(ks[10], (L, GMLP_GROUPS, GMLP_CHUNK, GMLP_CHUNK), GMLP_CHUNK ** -0.5),
        "b_spatial": 1.0 + nrm(ks[11], (L, GMLP_GROUPS, GMLP_CHUNK), 0.02),
        "w_branch_a": nrm(ks[12], (L, Q_DIM, D), Q_DIM ** -0.5),
        "w_branch_b": nrm(ks[13], (L, POOL_DIM, D), POOL_DIM ** -0.5),
        "w_branch_c": nrm(ks[14], (L, GMLP_DIM, D), GMLP_DIM ** -0.5),
        "w_out": nrm(ks[15], (L, D, D), D ** -0.5),
        "ffn_norm_gain": 1.0 + nrm(ks[16], (L, D), 0.02),
        "peer_w_query": nrm(ks[17], (L, D, PEER_HEADS * PEER_QDIM), D ** -0.5),
        "peer_sub_keys1": nrm(ks[18], (L, N_KEYS, PEER_HALF), PEER_HALF ** -0.5),
        "peer_sub_keys2": nrm(ks[19], (L, N_KEYS, PEER_HALF), PEER_HALF ** -0.5),
        "peer_expert_u": nrm(ks[20], (L, N_EXPERTS, D), D ** -0.5),
        "peer_expert_v": nrm(ks[21], (L, N_EXPERTS, D), PEER_HEADS ** -0.5),
    }


def reference(x, rel_bias, final_gain, attn_norm_gain, w_in, b_gate, attn_sinks, w_pool, pool_scale, gmlp_v_gain, w_spatial, b_spatial, w_branch_a, w_branch_b, w_branch_c, w_out, ffn_norm_gain, peer_w_query, peer_sub_keys1, peer_sub_keys2, peer_expert_u, peer_expert_v):
    B, S, D = x.shape
    for l in range(DEPTH):
        h = rmsnorm(x, attn_norm_gain[l])
        proj = h @ w_in[l]
        q, k, v, pz, gu, gv, g = jnp.split(proj, SPLIT_POINTS, axis=-1)
        gates = jax.nn.sigmoid((g + b_gate[l]).astype(jnp.float32)).astype(x.dtype).reshape(B, S, N_BRANCHES, D)
        ya = sliding_window_attention(q, k, v, attn_sinks[l], rel_bias) @ w_branch_a[l]
        yb = multiscale_pool(pz, w_pool[l], pool_scale[l]) @ w_branch_b[l]
        yc = chunked_spatial_gating(gu, gv, gmlp_v_gain[l], w_spatial[l], b_spatial[l]) @ w_branch_c[l]
        merged = gates[:, :, 0] * ya + gates[:, :, 1] * yb + gates[:, :, 2] * yc
        x = x + merged @ w_out[l]
        h2 = rmsnorm(x, ffn_norm_gain[l])
        x = x + peer_ffn(h2, peer_w_query[l], peer_sub_keys1[l], peer_sub_keys2[l], peer_expert_u[l], peer_expert_v[l])
    return rmsnorm(x, final_gain)
```

```python
import functools
import math

import jax
import jax.numpy as jnp
from jax import lax
from jax.experimental import pallas as pl
from jax.experimental.pallas import tpu as pltpu

D_MODEL = 1024
N_HEADS = 8
N_KV_HEADS = 2
HEAD_DIM = 64
GQA_GROUP = N_HEADS // N_KV_HEADS
WINDOW = 128
BLK = 128
Q_DIM = N_HEADS * HEAD_DIM
KV_DIM = N_KV_HEADS * HEAD_DIM
N_BUCKETS = 32
MAX_DISTANCE = 128
POOL_WINDOWS = (2, 4, 8, 16)
POOL_GROUP_DIM = 128
POOL_DIM = 512
POOL_TAIL = 16
GMLP_DIM = 512
GMLP_GROUPS = 4
GATE_DIM = 3 * D_MODEL
PEER_HEADS = 8
N_KEYS = 128
PEER_TOPK = 16
PEER_QDIM = 256
PEER_HALF = 128
PEER_SEL = PEER_HEADS * PEER_TOPK
EPS = 1e-6
NEG_INF = -1e30

SUBLANES = 8
LANES = 128
HALF_EXPERTS = (N_KEYS * N_KEYS) // 2
HALF_SHIFT = 13
VMEM_LIMIT = 56 * 1024 * 1024


def _rmsnorm(x, gain):
    return x * lax.rsqrt(jnp.mean(x * x, axis=-1, keepdims=True) + EPS) * gain


def _gelu(x):
    c = math.sqrt(2.0 / math.pi)
    return x * (0.5 * (1.0 + jnp.tanh(c * (x + 0.044715 * (x * x * x)))))


def _dot(a, b):
    return jnp.dot(a, b, preferred_element_type=jnp.float32)


def _dot_nt(a, b):
    return lax.dot_general(a, b, (((1,), (1,)), ((), ())), preferred_element_type=jnp.float32)


IN_TM = 256


def _in_proj_kernel(x_ref, gain_ref, w_ref, q_ref, kv_ref, pz_ref, guv_ref, g_ref):
    h = _rmsnorm(x_ref[...], gain_ref[...]).astype(jnp.bfloat16)
    o = 0
    for ref in (q_ref, kv_ref, pz_ref, guv_ref, g_ref):
        n = ref.shape[1]
        ref[...] = _dot(h, w_ref[:, o:o + n]).astype(ref.dtype)
        o += n


def _in_proj(x, gain, w_in):
    T = x.shape[0]
    widths = (Q_DIM, 2 * KV_DIM, POOL_DIM, 2 * GMLP_DIM, GATE_DIM)
    dtypes = (jnp.bfloat16, jnp.bfloat16, jnp.float32, jnp.float32, jnp.float32)
    return pl.pallas_call(
        _in_proj_kernel,
        grid=(T // IN_TM,),
        in_specs=[
            pl.BlockSpec((IN_TM, D_MODEL), lambda i: (i, 0)),
            pl.BlockSpec((1, D_MODEL), lambda i: (0, 0)),
            pl.BlockSpec(w_in.shape, lambda i: (0, 0)),
        ],
        out_specs=[pl.BlockSpec((IN_TM, n), lambda i: (i, 0)) for n in widths],
        out_shape=[jax.ShapeDtypeStruct((T, n), d) for n, d in zip(widths, dtypes)],
        compiler_params=pltpu.CompilerParams(
            dimension_semantics=("parallel",), vmem_limit_bytes=VMEM_LIMIT),
        name="in_proj",
    )(x, gain, w_in)


def _mixer_kernel(sinks_ref, x_ref, q_ref, kvc_ref, kvp_ref, pzc_ref, pzp_ref, guv_ref, g_ref,
                  bias_ref, wpool_ref, pscale_ref, vgain_ref, wsp_ref, bsp_ref,
                  wa_ref, wb_ref, wc_ref, wout_ref, bgate_ref, o_ref):
    i = pl.program_id(1)
    first = i == 0

    kvc = kvc_ref[...]
    kvp = kvp_ref[...]
    kk = jnp.concatenate([kvp[:, :KV_DIM], kvc[:, :KV_DIM]], axis=0)
    vv = jnp.concatenate([kvp[:, KV_DIM:], kvc[:, KV_DIM:]], axis=0)
    qi = lax.broadcasted_iota(jnp.int32, (BLK, 2 * BLK), 0)
    kj = lax.broadcasted_iota(jnp.int32, (BLK, 2 * BLK), 1)
    dist = qi + BLK - kj
    visible = (dist >= 0) & (dist < WINDOW) & jnp.logical_not(first & (kj < BLK))
    q = q_ref[...]
    heads = []
    for h in range(N_HEADS):
        g = h // GQA_GROUP
        qh = q[:, h * HEAD_DIM:(h + 1) * HEAD_DIM]
        kh = kk[:, g * HEAD_DIM:(g + 1) * HEAD_DIM]
        vh = vv[:, g * HEAD_DIM:(g + 1) * HEAD_DIM]
        s = _dot_nt(qh, kh) * (HEAD_DIM ** -0.5) + bias_ref[h]
        s = jnp.where(visible, s, NEG_INF)
        sink = sinks_ref[h]
        m = jnp.maximum(jnp.max(s, axis=-1, keepdims=True), sink)
        p = jnp.exp(s - m)
        denom = jnp.sum(p, axis=-1, keepdims=True) + jnp.exp(sink - m)
        heads.append(_dot(p.astype(jnp.bfloat16), vh) / denom)
    attn = jnp.concatenate(heads, axis=-1).astype(jnp.bfloat16)
    ya = _dot(attn, wa_ref[...])

    zc = pzc_ref[...]
    zp = jnp.where(first, 0.0, pzp_ref[...])
    ext = jnp.concatenate([zp, zc], axis=0)
    pos = i * BLK + lax.broadcasted_iota(jnp.int32, (BLK, 1), 0)
    pooled = []
    for gi, w in enumerate(POOL_WINDOWS):
        lo = gi * POOL_GROUP_DIM
        s = ext[:, lo:lo + POOL_GROUP_DIM]
        width = 1
        while width < w:
            s = s + pltpu.roll(s, width, 0)
            width *= 2
        count = jnp.minimum(pos + 1, w).astype(jnp.float32)
        d = s[POOL_TAIL:] / count - zc[:, lo:lo + POOL_GROUP_DIM]
        pooled.append(_dot(d.astype(jnp.bfloat16), wpool_ref[gi]))
    pool = (jnp.concatenate(pooled, axis=-1) * pscale_ref[...]).astype(jnp.bfloat16)
    yb = _dot(pool, wb_ref[...])

    guv = guv_ref[...]
    u = _gelu(guv[:, :GMLP_DIM])
    v = _rmsnorm(_gelu(guv[:, GMLP_DIM:]), vgain_ref[...]).astype(jnp.bfloat16)
    ti = lax.broadcasted_iota(jnp.int32, (BLK, BLK), 0)
    si = lax.broadcasted_iota(jnp.int32, (BLK, BLK), 1)
    bsp = bsp_ref[...]
    mixed = []
    gd = GMLP_DIM // GMLP_GROUPS
    for gi in range(GMLP_GROUPS):
        w = jnp.where(si <= ti, wsp_ref[gi], 0.0).astype(jnp.bfloat16)
        mixed.append(_dot(w, v[:, gi * gd:(gi + 1) * gd]) + bsp[:, gi:gi + 1])
    gm = (u * jnp.concatenate(mixed, axis=-1)).astype(jnp.bfloat16)
    yc = _dot(gm, wc_ref[...])

    gates = jax.nn.sigmoid(g_ref[...] + bgate_ref[...])
    merged = (gates[:, :D_MODEL] * ya + gates[:, D_MODEL:2 * D_MODEL] * yb
              + gates[:, 2 * D_MODEL:] * yc)
    o_ref[...] = x_ref[...] + _dot(merged.astype(jnp.bfloat16), wout_ref[...])


def _mixer(x, q, kv, pz, guv, g, sinks, bias, wpool, pscale, vgain, wsp, bsp,
           wa, wb, wc, wout, bgate, batch):
    T = x.shape[0]
    nb = T // batch // BLK
    tail_per_blk = BLK // POOL_TAIL

    def cur(b, i):
        return (b * nb + i, 0)

    def prev(b, i):
        return (b * nb + jnp.maximum(i - 1, 0), 0)

    def prev_tail(b, i):
        return (jnp.maximum((b * nb + i) * tail_per_blk - 1, 0), 0)

    def const(shape):
        return pl.BlockSpec(shape, lambda b, i: (0,) * len(shape))

    return pl.pallas_call(
        _mixer_kernel,
        grid=(batch, nb),
        in_specs=[
            pl.BlockSpec(memory_space=pltpu.SMEM),
            pl.BlockSpec((BLK, D_MODEL), cur),
            pl.BlockSpec((BLK, Q_DIM), cur),
            pl.BlockSpec((BLK, 2 * KV_DIM), cur),
            pl.BlockSpec((BLK, 2 * KV_DIM), prev),
            pl.BlockSpec((BLK, POOL_DIM), cur),
            pl.BlockSpec((POOL_TAIL, POOL_DIM), prev_tail),
            pl.BlockSpec((BLK, 2 * GMLP_DIM), cur),
            pl.BlockSpec((BLK, GATE_DIM), cur),
            const(bias.shape), const(wpool.shape), const(pscale.shape), const(vgain.shape),
            const(wsp.shape), const(bsp.shape), const(wa.shape), const(wb.shape),
            const(wc.shape), const(wout.shape), const(bgate.shape),
        ],
        out_specs=pl.BlockSpec((BLK, D_MODEL), cur),
        out_shape=jax.ShapeDtypeStruct((T, D_MODEL), jnp.float32),
        compiler_params=pltpu.CompilerParams(
            dimension_semantics=("parallel", "parallel"), vmem_limit_bytes=VMEM_LIMIT),
        name="mixer",
    )(sinks, x, q, kv, kv, pz, pz, guv, g, bias, wpool, pscale, vgain, wsp, bsp,
      wa, wb, wc, wout, bgate)


def _topk_rows(s, k):
    rows = s.shape[0]
    rid = lax.broadcasted_iota(jnp.int32, s.shape, 0)
    vals, ids = [], []
    for _ in range(k):
        m = jnp.max(s, axis=0, keepdims=True)
        sel = jnp.min(jnp.where(s == m, rid, rows), axis=0, keepdims=True)
        vals.append(m)
        ids.append(sel)
        s = jnp.where(rid == sel, -jnp.inf, s)
    return jnp.concatenate(vals, axis=0), jnp.concatenate(ids, axis=0)


def _pick_rows(table, which):
    out = jnp.zeros(which.shape, table.dtype)
    for r in range(table.shape[0]):
        out = jnp.where(which == r, table[r:r + 1], out)
    return out


def _route_kernel(x_ref, gain_ref, wq_ref, k1_ref, k2_ref, h_ref, addr_ref, shift_ref, gate_ref):
    h = _rmsnorm(x_ref[...], gain_ref[...])
    h_ref[...] = h
    qt = _dot_nt(wq_ref[...], h.astype(jnp.bfloat16)).astype(jnp.bfloat16)
    for hd in range(PEER_HEADS):
        base = hd * PEER_QDIM
        s1 = _dot(k1_ref[...], qt[base:base + PEER_HALF])
        s2 = _dot(k2_ref[...], qt[base + PEER_HALF:base + PEER_QDIM])
        v1, i1 = _topk_rows(s1, PEER_TOPK)
        v2, i2 = _topk_rows(s2, PEER_TOPK)
        cand = jnp.concatenate([v1[r:r + 1] + v2 for r in range(PEER_TOPK)], axis=0)
        top, ci = _topk_rows(cand, PEER_TOPK)
        expert = _pick_rows(i1, ci // PEER_TOPK) * N_KEYS + _pick_rows(i2, ci % PEER_TOPK)
        ex = jnp.exp(top - top[0:1])
        gate = ex / jnp.sum(ex, axis=0, keepdims=True)
        rows = pl.ds(hd * PEER_TOPK, PEER_TOPK)
        addr_ref[0, rows, :] = expert & (HALF_EXPERTS - 1)
        shift_ref[0, rows, :] = 16 - 16 * (expert >> HALF_SHIFT)
        gate_ref[0, rows, :] = gate


def _route(x, gain, wq_t, keys1, keys2):
    T = x.shape[0]
    nblk = T // BLK
    sel_spec = pl.BlockSpec((1, PEER_SEL, BLK), lambda i: (i, 0, 0))
    return pl.pallas_call(
        _route_kernel,
        grid=(nblk,),
        in_specs=[
            pl.BlockSpec((BLK, D_MODEL), lambda i: (i, 0)),
            pl.BlockSpec((1, D_MODEL), lambda i: (0, 0)),
            pl.BlockSpec(wq_t.shape, lambda i: (0, 0)),
            pl.BlockSpec(keys1.shape, lambda i: (0, 0)),
            pl.BlockSpec(keys2.shape, lambda i: (0, 0)),
        ],
        out_specs=[pl.BlockSpec((BLK, D_MODEL), lambda i: (i, 0)), sel_spec, sel_spec, sel_spec],
        out_shape=[
            jax.ShapeDtypeStruct((T, D_MODEL), jnp.float32),
            jax.ShapeDtypeStruct((nblk, PEER_SEL, BLK), jnp.int32),
            jax.ShapeDtypeStruct((nblk, PEER_SEL, BLK), jnp.int32),
            jax.ShapeDtypeStruct((nblk, PEER_SEL, BLK), jnp.float32),
        ],
        compiler_params=pltpu.CompilerParams(
            dimension_semantics=("parallel",), vmem_limit_bytes=VMEM_LIMIT),
        name="route",
    )(x, gain, wq_t, keys1, keys2)


def _expert_row(tab_ref, addr, shift):
    word = tab_ref[addr]
    return lax.bitcast_convert_type(lax.shift_left(word, shift) & jnp.int32(-65536), jnp.float32)


def _sublane_sums(parts):
    sub = lax.broadcasted_iota(jnp.int32, (SUBLANES, LANES), 0)
    step = 1
    while len(parts) > 1:
        pick = (sub & step) != 0
        nxt = []
        for a, b in zip(parts[0::2], parts[1::2]):
            nxt.append(jnp.where(pick, a, b) + pltpu.roll(jnp.where(pick, b, a), step, 0))
        parts = nxt
        step *= 2
    return parts[0]


def _load_smem(src_hbm, dst_smem, sem, blk):
    copy = pltpu.make_async_copy(src_hbm.at[blk], dst_smem, sem)
    copy.start()
    copy.wait()


def _peer_u_kernel(addr_hbm, shift_hbm, h_ref, gate_ref, tab_ref, coeff_ref,
                   addr_s, shift_s, sems):
    blk = pl.program_id(0)
    _load_smem(addr_hbm, addr_s, sems.at[0], blk)
    _load_smem(shift_hbm, shift_s, sems.at[1], blk)
    lane = lax.broadcasted_iota(jnp.int32, (PEER_SEL, BLK), 1)

    def token(t, acc):
        ht = h_ref[t]
        tiles = []
        for grp in range(PEER_SEL // SUBLANES):
            parts = []
            for j in range(SUBLANES):
                k = grp * SUBLANES + (SUBLANES - 1 - j)
                parts.append(_expert_row(tab_ref, addr_s[k, t], shift_s[k, t]) * ht)
            tiles.append(_sublane_sums(parts))
        act = jnp.sum(jnp.concatenate(tiles, axis=0), axis=1, keepdims=True)
        return jnp.where(lane == t, act, acc)

    act = lax.fori_loop(0, BLK, token, jnp.zeros((PEER_SEL, BLK), jnp.float32))
    coeff_ref[0] = gate_ref[0] * _gelu(act)


def _peer_v_kernel(addr_hbm, shift_hbm, coeff_hbm, x_ref, tab_ref, o_ref,
                   addr_s, shift_s, coeff_s, sems):
    blk = pl.program_id(0)
    _load_smem(addr_hbm, addr_s, sems.at[0], blk)
    _load_smem(shift_hbm, shift_s, sems.at[1], blk)
    _load_smem(coeff_hbm, coeff_s, sems.at[2], blk)
    n_acc = 4

    def token(t, carry):
        accs = [jnp.zeros((SUBLANES, LANES), jnp.float32) for _ in range(n_acc)]
        for k in range(PEER_SEL):
            row = _expert_row(tab_ref, addr_s[k, t], shift_s[k, t])
            accs[k % n_acc] = accs[k % n_acc] + coeff_s[k, t] * row
        o_ref[t] = x_ref[t] + ((accs[0] + accs[1]) + (accs[2] + accs[3]))
        return carry

    lax.fori_loop(0, BLK, token, 0)


def _table_spec(tab):
    return pl.BlockSpec(tab.shape, lambda i: (0, 0, 0), pipeline_mode=pl.Buffered(1))


def _peer_u(addr, shift, h3, gate, tab):
    nblk = addr.shape[0]
    sel_spec = pl.BlockSpec((1, PEER_SEL, BLK), lambda i: (i, 0, 0))
    return pl.pallas_call(
        _peer_u_kernel,
        grid=(nblk,),
        in_specs=[
            pl.BlockSpec(memory_space=pl.ANY),
            pl.BlockSpec(memory_space=pl.ANY),
            pl.BlockSpec((BLK, SUBLANES, LANES), lambda i: (i, 0, 0)),
            sel_spec,
            _table_spec(tab),
        ],
        out_specs=sel_spec,
        out_shape=jax.ShapeDtypeStruct((nblk, PEER_SEL, BLK), jnp.float32),
        scratch_shapes=[
            pltpu.SMEM((PEER_SEL, BLK), jnp.int32),
            pltpu.SMEM((PEER_SEL, BLK), jnp.int32),
            pltpu.SemaphoreType.DMA((2,)),
        ],
        compiler_params=pltpu.CompilerParams(
            dimension_semantics=("arbitrary",), vmem_limit_bytes=VMEM_LIMIT),
        name="peer_u",
    )(addr, shift, h3, gate, tab)


def _peer_v(addr, shift, coeff, x3, tab):
    nblk = addr.shape[0]
    tok_spec = pl.BlockSpec((BLK, SUBLANES, LANES), lambda i: (i, 0, 0))
    return pl.pallas_call(
        _peer_v_kernel,
        grid=(nblk,),
        in_specs=[
            pl.BlockSpec(memory_space=pl.ANY),
            pl.BlockSpec(memory_space=pl.ANY),
            pl.BlockSpec(memory_space=pl.ANY),
            tok_spec,
            _table_spec(tab),
        ],
        out_specs=tok_spec,
        out_shape=jax.ShapeDtypeStruct(x3.shape, jnp.float32),
        scratch_shapes=[
            pltpu.SMEM((PEER_SEL, BLK), jnp.int32),
            pltpu.SMEM((PEER_SEL, BLK), jnp.int32),
            pltpu.SMEM((PEER_SEL, BLK), jnp.float32),
            pltpu.SemaphoreType.DMA((3,)),
        ],
        compiler_params=pltpu.CompilerParams(
            dimension_semantics=("arbitrary",), vmem_limit_bytes=VMEM_LIMIT),
        name="peer_v",
    )(addr, shift, coeff, x3, tab)


NORM_TM = 512


def _final_norm_kernel(x_ref, gain_ref, o_ref):
    o_ref[...] = _rmsnorm(x_ref[...], gain_ref[...])


def _final_norm(x, gain):
    T = x.shape[0]
    return pl.pallas_call(
        _final_norm_kernel,
        grid=(T // NORM_TM,),
        in_specs=[pl.BlockSpec((NORM_TM, D_MODEL), lambda i: (i, 0)),
                  pl.BlockSpec((1, D_MODEL), lambda i: (0, 0))],
        out_specs=pl.BlockSpec((NORM_TM, D_MODEL), lambda i: (i, 0)),
        out_shape=jax.ShapeDtypeStruct((T, D_MODEL), jnp.float32),
        compiler_params=pltpu.CompilerParams(dimension_semantics=("parallel",)),
        name="final_norm",
    )(x, gain)


def _t5_causal_bucket(dist):
    n = jnp.maximum(dist, 0)
    max_exact = N_BUCKETS // 2
    nf = jnp.maximum(n, 1).astype(jnp.float32)
    large = max_exact + (jnp.log(nf / max_exact) / math.log(MAX_DISTANCE / max_exact)
                         * (N_BUCKETS - max_exact)).astype(jnp.int32)
    large = jnp.minimum(large, N_BUCKETS - 1)
    return jnp.where(n < max_exact, n, large)


def _attention_bias(rel_bias):
    qi = jnp.arange(BLK)[:, None]
    kj = jnp.arange(2 * BLK)[None, :]
    bias = rel_bias[_t5_causal_bucket(qi + BLK - kj)].astype(jnp.float32)
    return bias.transpose(2, 0, 1)


def _pack_table(tab):
    bits = lax.bitcast_convert_type(tab.astype(jnp.bfloat16), jnp.uint16).astype(jnp.uint32)
    words = bits[:HALF_EXPERTS] | (bits[HALF_EXPERTS:] << 16)
    return lax.bitcast_convert_type(words, jnp.int32).reshape(HALF_EXPERTS, SUBLANES, LANES)


def kernel(x, rel_bias, final_gain, attn_norm_gain, w_in, b_gate, attn_sinks, w_pool, pool_scale, gmlp_v_gain, w_spatial, b_spatial, w_branch_a, w_branch_b, w_branch_c, w_out, ffn_norm_gain, peer_w_query, peer_sub_keys1, peer_sub_keys2, peer_expert_u, peer_expert_v):
    B, S, D = x.shape
    T = B * S
    depth = w_in.shape[0]
    bf16 = jnp.bfloat16
    bias = _attention_bias(rel_bias)
    xt = x.reshape(T, D)
    for l in range(depth):
        q, kv, pz, guv, g = _in_proj(xt, attn_norm_gain[l][None], w_in[l].astype(bf16))
        xt = _mixer(xt, q, kv, pz, guv, g, attn_sinks[l], bias,
                    w_pool[l].astype(bf16), pool_scale[l][None], gmlp_v_gain[l][None],
                    w_spatial[l], b_spatial[l].T,
                    w_branch_a[l].astype(bf16), w_branch_b[l].astype(bf16),
                    w_branch_c[l].astype(bf16), w_out[l].astype(bf16), b_gate[l][None], B)
        h, addr, shift, gate = _route(xt, ffn_norm_gain[l][None], peer_w_query[l].T.astype(bf16),
                                      peer_sub_keys1[l].astype(bf16), peer_sub_keys2[l].astype(bf16))
        coeff = _peer_u(addr, shift, h.reshape(T, SUBLANES, LANES), gate,
                        _pack_table(peer_expert_u[l]))
        xt = _peer_v(addr, shift, coeff, xt.reshape(T, SUBLANES, LANES),
                     _pack_table(peer_expert_v[l])).reshape(T, D)
    return _final_norm(xt, final_gain[None]).reshape(B, S, D)
```

```python
import functools
import math

import jax
import jax.numpy as jnp
from jax import lax
from jax.experimental import pallas as pl
from jax.experimental.pallas import tpu as pltpu

D_MODEL = 1024
N_HEADS = 8
N_KV_HEADS = 2
HEAD_DIM = 64
GQA_GROUP = N_HEADS // N_KV_HEADS
WINDOW = 128
BLK = 128
Q_DIM = N_HEADS * HEAD_DIM
KV_DIM = N_KV_HEADS * HEAD_DIM
N_BUCKETS = 32
MAX_DISTANCE = 128
POOL_WINDOWS = (2, 4, 8, 16)
POOL_GROUP_DIM = 128
POOL_DIM = 512
POOL_TAIL = 16
GMLP_DIM = 512
GMLP_GROUPS = 4
GATE_DIM = 3 * D_MODEL
PEER_HEADS = 8
N_KEYS = 128
PEER_TOPK = 16
PEER_QDIM = 256
PEER_HALF = 128
PEER_SEL = PEER_HEADS * PEER_TOPK
EPS = 1e-6
NEG_INF = -1e30

SUBLANES = 8
LANES = 128
N_EXPERTS = N_KEYS * N_KEYS
HALF_ROWS = SUBLANES // 2
VMEM_LIMIT = 56 * 1024 * 1024


def _rmsnorm(x, gain):
    return x * lax.rsqrt(jnp.mean(x * x, axis=-1, keepdims=True) + EPS) * gain


def _gelu(x):
    c = math.sqrt(2.0 / math.pi)
    return x * (0.5 * (1.0 + jnp.tanh(c * (x + 0.044715 * (x * x * x)))))


def _dot(a, b):
    return jnp.dot(a, b, preferred_element_type=jnp.float32)


def _dot_nt(a, b):
    return lax.dot_general(a, b, (((1,), (1,)), ((), ())), preferred_element_type=jnp.float32)


IN_TM = 256


def _in_proj_kernel(x_ref, gain_ref, w_ref, q_ref, kv_ref, pz_ref, guv_ref, g_ref):
    h = _rmsnorm(x_ref[...], gain_ref[...]).astype(jnp.bfloat16)
    o = 0
    for ref in (q_ref, kv_ref, pz_ref, guv_ref, g_ref):
        n = ref.shape[1]
        ref[...] = _dot(h, w_ref[:, o:o + n]).astype(ref.dtype)
        o += n


def _in_proj(x, gain, w_in):
    T = x.shape[0]
    widths = (Q_DIM, 2 * KV_DIM, POOL_DIM, 2 * GMLP_DIM, GATE_DIM)
    dtypes = (jnp.bfloat16, jnp.bfloat16, jnp.float32, jnp.float32, jnp.float32)
    return pl.pallas_call(
        _in_proj_kernel,
        grid=(T // IN_TM,),
        in_specs=[
            pl.BlockSpec((IN_TM, D_MODEL), lambda i: (i, 0)),
            pl.BlockSpec((1, D_MODEL), lambda i: (0, 0)),
            pl.BlockSpec(w_in.shape, lambda i: (0, 0)),
        ],
        out_specs=[pl.BlockSpec((IN_TM, n), lambda i: (i, 0)) for n in widths],
        out_shape=[jax.ShapeDtypeStruct((T, n), d) for n, d in zip(widths, dtypes)],
        compiler_params=pltpu.CompilerParams(
            dimension_semantics=("parallel",), vmem_limit_bytes=VMEM_LIMIT),
        name="in_proj",
    )(x, gain, w_in)


def _mixer_kernel(sinks_ref, x_ref, q_ref, kvc_ref, kvp_ref, pzc_ref, pzp_ref, guv_ref, g_ref,
                  bias_ref, wpool_ref, pscale_ref, vgain_ref, wsp_ref, bsp_ref,
                  wa_ref, wb_ref, wc_ref, wout_ref, bgate_ref, o_ref):
    i = pl.program_id(1)
    first = i == 0

    kvc = kvc_ref[...]
    kvp = kvp_ref[...]
    kk = jnp.concatenate([kvp[:, :KV_DIM], kvc[:, :KV_DIM]], axis=0)
    vv = jnp.concatenate([kvp[:, KV_DIM:], kvc[:, KV_DIM:]], axis=0)
    qi = lax.broadcasted_iota(jnp.int32, (BLK, 2 * BLK), 0)
    kj = lax.broadcasted_iota(jnp.int32, (BLK, 2 * BLK), 1)
    dist = qi + BLK - kj
    visible = (dist >= 0) & (dist < WINDOW) & jnp.logical_not(first & (kj < BLK))
    q = q_ref[...]
    heads = []
    for h in range(N_HEADS):
        g = h // GQA_GROUP
        qh = q[:, h * HEAD_DIM:(h + 1) * HEAD_DIM]
        kh = kk[:, g * HEAD_DIM:(g + 1) * HEAD_DIM]
        vh = vv[:, g * HEAD_DIM:(g + 1) * HEAD_DIM]
        s = _dot_nt(qh, kh) * (HEAD_DIM ** -0.5) + bias_ref[h]
        s = jnp.where(visible, s, NEG_INF)
        sink = sinks_ref[h]
        m = jnp.maximum(jnp.max(s, axis=-1, keepdims=True), sink)
        p = jnp.exp(s - m)
        denom = jnp.sum(p, axis=-1, keepdims=True) + jnp.exp(sink - m)
        heads.append(_dot(p.astype(jnp.bfloat16), vh) / denom)
    attn = jnp.concatenate(heads, axis=-1).astype(jnp.bfloat16)
    ya = _dot(attn, wa_ref[...])

    zc = pzc_ref[...]
    zp = jnp.where(first, 0.0, pzp_ref[...])
    ext = jnp.concatenate([zp, zc], axis=0)
    pos = i * BLK + lax.broadcasted_iota(jnp.int32, (BLK, 1), 0)
    pooled = []
    for gi, w in enumerate(POOL_WINDOWS):
        lo = gi * POOL_GROUP_DIM
        s = ext[:, lo:lo + POOL_GROUP_DIM]
        width = 1
        while width < w:
            s = s + pltpu.roll(s, width, 0)
            width *= 2
        count = jnp.minimum(pos + 1, w).astype(jnp.float32)
        d = s[POOL_TAIL:] / count - zc[:, lo:lo + POOL_GROUP_DIM]
        pooled.append(_dot(d.astype(jnp.bfloat16), wpool_ref[gi]))
    pool = (jnp.concatenate(pooled, axis=-1) * pscale_ref[...]).astype(jnp.bfloat16)
    yb = _dot(pool, wb_ref[...])

    guv = guv_ref[...]
    u = _gelu(guv[:, :GMLP_DIM])
    v = _rmsnorm(_gelu(guv[:, GMLP_DIM:]), vgain_ref[...]).astype(jnp.bfloat16)
    ti = lax.broadcasted_iota(jnp.int32, (BLK, BLK), 0)
    si = lax.broadcasted_iota(jnp.int32, (BLK, BLK), 1)
    bsp = bsp_ref[...]
    mixed = []
    gd = GMLP_DIM // GMLP_GROUPS
    for gi in range(GMLP_GROUPS):
        w = jnp.where(si <= ti, wsp_ref[gi], 0.0).astype(jnp.bfloat16)
        mixed.append(_dot(w, v[:, gi * gd:(gi + 1) * gd]) + bsp[:, gi:gi + 1])
    gm = (u * jnp.concatenate(mixed, axis=-1)).astype(jnp.bfloat16)
    yc = _dot(gm, wc_ref[...])

    gates = jax.nn.sigmoid(g_ref[...] + bgate_ref[...])
    merged = (gates[:, :D_MODEL] * ya + gates[:, D_MODEL:2 * D_MODEL] * yb
              + gates[:, 2 * D_MODEL:] * yc)
    o_ref[...] = x_ref[...] + _dot(merged.astype(jnp.bfloat16), wout_ref[...])


def _mixer(x, q, kv, pz, guv, g, sinks, bias, wpool, pscale, vgain, wsp, bsp,
           wa, wb, wc, wout, bgate, batch):
    T = x.shape[0]
    nb = T // batch // BLK
    tail_per_blk = BLK // POOL_TAIL

    def cur(b, i):
        return (b * nb + i, 0)

    def prev(b, i):
        return (b * nb + jnp.maximum(i - 1, 0), 0)

    def prev_tail(b, i):
        return (jnp.maximum((b * nb + i) * tail_per_blk - 1, 0), 0)

    def const(shape):
        return pl.BlockSpec(shape, lambda b, i: (0,) * len(shape))

    return pl.pallas_call(
        _mixer_kernel,
        grid=(batch, nb),
        in_specs=[
            pl.BlockSpec(memory_space=pltpu.SMEM),
            pl.BlockSpec((BLK, D_MODEL), cur),
            pl.BlockSpec((BLK, Q_DIM), cur),
            pl.BlockSpec((BLK, 2 * KV_DIM), cur),
            pl.BlockSpec((BLK, 2 * KV_DIM), prev),
            pl.BlockSpec((BLK, POOL_DIM), cur),
            pl.BlockSpec((POOL_TAIL, POOL_DIM), prev_tail),
            pl.BlockSpec((BLK, 2 * GMLP_DIM), cur),
            pl.BlockSpec((BLK, GATE_DIM), cur),
            const(bias.shape), const(wpool.shape), const(pscale.shape), const(vgain.shape),
            const(wsp.shape), const(bsp.shape), const(wa.shape), const(wb.shape),
            const(wc.shape), const(wout.shape), const(bgate.shape),
        ],
        out_specs=pl.BlockSpec((BLK, D_MODEL), cur),
        out_shape=jax.ShapeDtypeStruct((T, D_MODEL), jnp.float32),
        compiler_params=pltpu.CompilerParams(
            dimension_semantics=("parallel", "parallel"), vmem_limit_bytes=VMEM_LIMIT),
        name="mixer",
    )(sinks, x, q, kv, kv, pz, pz, guv, g, bias, wpool, pscale, vgain, wsp, bsp,
      wa, wb, wc, wout, bgate)


def _topk_rows(s, k):
    rows = s.shape[0]
    rid = lax.broadcasted_iota(jnp.int32, s.shape, 0)
    vals, ids = [], []
    for _ in range(k):
        m = jnp.max(s, axis=0, keepdims=True)
        sel = jnp.min(jnp.where(s == m, rid, rows), axis=0, keepdims=True)
        vals.append(m)
        ids.append(sel)
        s = jnp.where(rid == sel, -jnp.inf, s)
    return jnp.concatenate(vals, axis=0), jnp.concatenate(ids, axis=0)


def _pick_rows(table, which):
    out = jnp.zeros(which.shape, table.dtype)
    for r in range(table.shape[0]):
        out = jnp.where(which == r, table[r:r + 1], out)
    return out


def _route_kernel(x_ref, gain_ref, wq_ref, k1_ref, k2_ref, h_ref, expert_ref, gate_ref):
    h = _rmsnorm(x_ref[...], gain_ref[...])
    h_ref[...] = h
    qt = _dot_nt(wq_ref[...], h.astype(jnp.bfloat16)).astype(jnp.bfloat16)
    experts = []
    for hd in range(PEER_HEADS):
        base = hd * PEER_QDIM
        s1 = _dot(k1_ref[...], qt[base:base + PEER_HALF])
        s2 = _dot(k2_ref[...], qt[base + PEER_HALF:base + PEER_QDIM])
        v1, i1 = _topk_rows(s1, PEER_TOPK)
        v2, i2 = _topk_rows(s2, PEER_TOPK)
        cand = jnp.concatenate([v1[r:r + 1] + v2 for r in range(PEER_TOPK)], axis=0)
        top, ci = _topk_rows(cand, PEER_TOPK)
        experts.append(_pick_rows(i1, ci // PEER_TOPK) * N_KEYS + _pick_rows(i2, ci % PEER_TOPK))
        ex = jnp.exp(top - top[0:1])
        gate_ref[0, pl.ds(hd * PEER_TOPK, PEER_TOPK), :] = ex / jnp.sum(ex, axis=0, keepdims=True)
    expert_ref[0] = jnp.concatenate(experts, axis=0).T


def _route(x, gain, wq_t, keys1, keys2):
    T = x.shape[0]
    nblk = T // BLK
    sel_spec = pl.BlockSpec((1, PEER_SEL, BLK), lambda i: (i, 0, 0))
    tok_spec = pl.BlockSpec((1, BLK, PEER_SEL), lambda i: (i, 0, 0))
    return pl.pallas_call(
        _route_kernel,
        grid=(nblk,),
        in_specs=[
            pl.BlockSpec((BLK, D_MODEL), lambda i: (i, 0)),
            pl.BlockSpec((1, D_MODEL), lambda i: (0, 0)),
            pl.BlockSpec(wq_t.shape, lambda i: (0, 0)),
            pl.BlockSpec(keys1.shape, lambda i: (0, 0)),
            pl.BlockSpec(keys2.shape, lambda i: (0, 0)),
        ],
        out_specs=[pl.BlockSpec((BLK, D_MODEL), lambda i: (i, 0)), tok_spec, sel_spec],
        out_shape=[
            jax.ShapeDtypeStruct((T, D_MODEL), jnp.float32),
            jax.ShapeDtypeStruct((nblk, BLK, PEER_SEL), jnp.int32),
            jax.ShapeDtypeStruct((nblk, PEER_SEL, BLK), jnp.float32),
        ],
        compiler_params=pltpu.CompilerParams(
            dimension_semantics=("parallel",), vmem_limit_bytes=VMEM_LIMIT),
        name="route",
    )(x, gain, wq_t, keys1, keys2)


def _row_pair(tab_ref, ea, eb):
    words = jnp.concatenate([tab_ref[ea], tab_ref[eb]], axis=0)
    lo = lax.bitcast_convert_type(lax.shift_left(words, 16), jnp.float32)
    hi = lax.bitcast_convert_type(words & jnp.int32(-65536), jnp.float32)
    return lo, hi


def _half_sums(q, sub):
    odd = (sub & 1) != 0
    c0 = jnp.where(odd, q[0] + pltpu.roll(q[0], 1, 0), q[1] + pltpu.roll(q[1], 7, 0))
    c1 = jnp.where(odd, q[2] + pltpu.roll(q[2], 1, 0), q[3] + pltpu.roll(q[3], 7, 0))
    return jnp.where((sub & 2) != 0, c0 + pltpu.roll(c0, 2, 0), c1 + pltpu.roll(c1, 6, 0))


_GROUP_PAIRS = ((3, 7), (2, 6), (1, 5), (0, 4))


def _load_smem(src_hbm, dst_smem, sem, blk):
    copy = pltpu.make_async_copy(src_hbm.at[blk], dst_smem, sem)
    copy.start()
    copy.wait()


def _peer_u_kernel(expert_hbm, h_ref, gate_ref, tab_ref, coeff_ref, expert_s, sem):
    _load_smem(expert_hbm, expert_s, sem, pl.program_id(0))
    sub = lax.broadcasted_iota(jnp.int32, (SUBLANES, LANES), 0)
    lane = lax.broadcasted_iota(jnp.int32, (PEER_SEL, BLK), 1)

    def token(t, acc):
        ht = h_ref[t]
        h_lo = jnp.concatenate([ht[:HALF_ROWS], ht[:HALF_ROWS]], axis=0)
        h_hi = jnp.concatenate([ht[HALF_ROWS:], ht[HALF_ROWS:]], axis=0)
        tiles = []
        for grp in range(PEER_SEL // SUBLANES):
            q = []
            for a, b in _GROUP_PAIRS:
                lo, hi = _row_pair(tab_ref, expert_s[t, grp * SUBLANES + a],
                                   expert_s[t, grp * SUBLANES + b])
                q.append(lo * h_lo + hi * h_hi)
            tiles.append(_half_sums(q, sub))
        act = jnp.sum(jnp.concatenate(tiles, axis=0), axis=1, keepdims=True)
        return jnp.where(lane == t, act, acc)

    act = lax.fori_loop(0, BLK, token, jnp.zeros((PEER_SEL, BLK), jnp.float32))
    coeff_ref[0] = (gate_ref[0] * _gelu(act)).T


def _peer_v_kernel(expert_hbm, coeff_hbm, x_ref, tab_ref, o_ref, expert_s, coeff_s, sems):
    blk = pl.program_id(0)
    _load_smem(expert_hbm, expert_s, sems.at[0], blk)
    _load_smem(coeff_hbm, coeff_s, sems.at[1], blk)
    first_half = lax.broadcasted_iota(jnp.int32, (SUBLANES, LANES), 0) < HALF_ROWS
    n_acc = 2

    def token(t, carry):
        acc_lo = [jnp.zeros((SUBLANES, LANES), jnp.float32) for _ in range(n_acc)]
        acc_hi = [jnp.zeros((SUBLANES, LANES), jnp.float32) for _ in range(n_acc)]
        for p in range(PEER_SEL // 2):
            lo, hi = _row_pair(tab_ref, expert_s[t, 2 * p], expert_s[t, 2 * p + 1])
            c = jnp.where(first_half, coeff_s[t, 2 * p], coeff_s[t, 2 * p + 1])
            acc_lo[p % n_acc] = acc_lo[p % n_acc] + c * lo
            acc_hi[p % n_acc] = acc_hi[p % n_acc] + c * hi
        y_lo = acc_lo[0] + acc_lo[1]
        y_hi = acc_hi[0] + acc_hi[1]
        y = jnp.concatenate([y_lo[:HALF_ROWS] + y_lo[HALF_ROWS:],
                             y_hi[:HALF_ROWS] + y_hi[HALF_ROWS:]], axis=0)
        o_ref[t] = x_ref[t] + y
        return carry

    lax.fori_loop(0, BLK, token, 0)


def _table_spec(tab):
    return pl.BlockSpec(tab.shape, lambda i: (0, 0, 0), pipeline_mode=pl.Buffered(1))


def _peer_u(expert, h3, gate, tab):
    nblk = expert.shape[0]
    return pl.pallas_call(
        _peer_u_kernel,
        grid=(nblk,),
        in_specs=[
            pl.BlockSpec(memory_space=pl.ANY),
            pl.BlockSpec((BLK, SUBLANES, LANES), lambda i: (i, 0, 0)),
            pl.BlockSpec((1, PEER_SEL, BLK), lambda i: (i, 0, 0)),
            _table_spec(tab),
        ],
        out_specs=pl.BlockSpec((1, BLK, PEER_SEL), lambda i: (i, 0, 0)),
        out_shape=jax.ShapeDtypeStruct((nblk, BLK, PEER_SEL), jnp.float32),
        scratch_shapes=[
            pltpu.SMEM((BLK, PEER_SEL), jnp.int32),
            pltpu.SemaphoreType.DMA,
        ],
        compiler_params=pltpu.CompilerParams(
            dimension_semantics=("arbitrary",), vmem_limit_bytes=VMEM_LIMIT),
        name="peer_u",
    )(expert, h3, gate, tab)


def _peer_v(expert, coeff, x3, tab):
    nblk = expert.shape[0]
    tok_spec = pl.BlockSpec((BLK, SUBLANES, LANES), lambda i: (i, 0, 0))
    return pl.pallas_call(
        _peer_v_kernel,
        grid=(nblk,),
        in_specs=[
            pl.BlockSpec(memory_space=pl.ANY),
            pl.BlockSpec(memory_space=pl.ANY),
            tok_spec,
            _table_spec(tab),
        ],
        out_specs=tok_spec,
        out_shape=jax.ShapeDtypeStruct(x3.shape, jnp.float32),
        scratch_shapes=[
            pltpu.SMEM((BLK, PEER_SEL), jnp.int32),
            pltpu.SMEM((BLK, PEER_SEL), jnp.float32),
            pltpu.SemaphoreType.DMA((2,)),
        ],
        compiler_params=pltpu.CompilerParams(
            dimension_semantics=("arbitrary",), vmem_limit_bytes=VMEM_LIMIT),
        name="peer_v",
    )(expert, coeff, x3, tab)


NORM_TM = 512


def _final_norm_kernel(x_ref, gain_ref, o_ref):
    o_ref[...] = _rmsnorm(x_ref[...], gain_ref[...])


def _final_norm(x, gain):
    T = x.shape[0]
    return pl.pallas_call(
        _final_norm_kernel,
        grid=(T // NORM_TM,),
        in_specs=[pl.BlockSpec((NORM_TM, D_MODEL), lambda i: (i, 0)),
                  pl.BlockSpec((1, D_MODEL), lambda i: (0, 0))],
        out_specs=pl.BlockSpec((NORM_TM, D_MODEL), lambda i: (i, 0)),
        out_shape=jax.ShapeDtypeStruct((T, D_MODEL), jnp.float32),
        compiler_params=pltpu.CompilerParams(dimension_semantics=("parallel",)),
        name="final_norm",
    )(x, gain)


def _t5_causal_bucket(dist):
    n = jnp.maximum(dist, 0)
    max_exact = N_BUCKETS // 2
    nf = jnp.maximum(n, 1).astype(jnp.float32)
    large = max_exact + (jnp.log(nf / max_exact) / math.log(MAX_DISTANCE / max_exact)
                         * (N_BUCKETS - max_exact)).astype(jnp.int32)
    large = jnp.minimum(large, N_BUCKETS - 1)
    return jnp.where(n < max_exact, n, large)


def _attention_bias(rel_bias):
    qi = jnp.arange(BLK)[:, None]
    kj = jnp.arange(2 * BLK)[None, :]
    bias = rel_bias[_t5_causal_bucket(qi + BLK - kj)].astype(jnp.float32)
    return bias.transpose(2, 0, 1)


def _pack_table(tab):
    bits = lax.bitcast_convert_type(tab.astype(jnp.bfloat16), jnp.uint16).astype(jnp.uint32)
    words = bits[:, :D_MODEL // 2] | (bits[:, D_MODEL // 2:] << 16)
    return lax.bitcast_convert_type(words, jnp.int32).reshape(N_EXPERTS, HALF_ROWS, LANES)


def kernel(x, rel_bias, final_gain, attn_norm_gain, w_in, b_gate, attn_sinks, w_pool, pool_scale, gmlp_v_gain, w_spatial, b_spatial, w_branch_a, w_branch_b, w_branch_c, w_out, ffn_norm_gain, peer_w_query, peer_sub_keys1, peer_sub_keys2, peer_expert_u, peer_expert_v):
    B, S, D = x.shape
    T = B * S
    depth = w_in.shape[0]
    bf16 = jnp.bfloat16
    bias = _attention_bias(rel_bias)
    xt = x.reshape(T, D)
    for l in range(depth):
        q, kv, pz, guv, g = _in_proj(xt, attn_norm_gain[l][None], w_in[l].astype(bf16))
        xt = _mixer(xt, q, kv, pz, guv, g, attn_sinks[l], bias,
                    w_pool[l].astype(bf16), pool_scale[l][None], gmlp_v_gain[l][None],
                    w_spatial[l], b_spatial[l].T,
                    w_branch_a[l].astype(bf16), w_branch_b[l].astype(bf16),
                    w_branch_c[l].astype(bf16), w_out[l].astype(bf16), b_gate[l][None], B)
        h, expert, gate = _route(xt, ffn_norm_gain[l][None], peer_w_query[l].T.astype(bf16),
                                 peer_sub_keys1[l].astype(bf16), peer_sub_keys2[l].astype(bf16))
        coeff = _peer_u(expert, h.reshape(T, SUBLANES, LANES), gate, _pack_table(peer_expert_u[l]))
        xt = _peer_v(expert, coeff, xt.reshape(T, SUBLANES, LANES),
                     _pack_table(peer_expert_v[l])).reshape(T, D)
    return _final_norm(xt, final_gain[None]).reshape(B, S, D)
```

```python
import functools
import math

import jax
import jax.numpy as jnp
from jax import lax
from jax.experimental import pallas as pl
from jax.experimental.pallas import tpu as pltpu

D_MODEL = 1024
N_HEADS = 8
N_KV_HEADS = 2
HEAD_DIM = 64
GQA_GROUP = N_HEADS // N_KV_HEADS
WINDOW = 128
BLK = 128
Q_DIM = N_HEADS * HEAD_DIM
KV_DIM = N_KV_HEADS * HEAD_DIM
N_BUCKETS = 32
MAX_DISTANCE = 128
POOL_WINDOWS = (2, 4, 8, 16)
POOL_GROUP_DIM = 128
POOL_DIM = 512
POOL_TAIL = 16
GMLP_DIM = 512
GMLP_GROUPS = 4
GATE_DIM = 3 * D_MODEL
PEER_HEADS = 8
N_KEYS = 128
PEER_TOPK = 16
PEER_QDIM = 256
PEER_HALF = 128
PEER_SEL = PEER_HEADS * PEER_TOPK
EPS = 1e-6
NEG_INF = -1e30

SUBLANES = 8
LANES = 128
N_EXPERTS = N_KEYS * N_KEYS
HALF_ROWS = SUBLANES // 2
VMEM_LIMIT = 56 * 1024 * 1024


def _rmsnorm(x, gain):
    return x * lax.rsqrt(jnp.mean(x * x, axis=-1, keepdims=True) + EPS) * gain


def _gelu(x):
    c = math.sqrt(2.0 / math.pi)
    return x * (0.5 * (1.0 + jnp.tanh(c * (x + 0.044715 * (x * x * x)))))


def _dot(a, b):
    return jnp.dot(a, b, preferred_element_type=jnp.float32)


def _dot_nt(a, b):
    return lax.dot_general(a, b, (((1,), (1,)), ((), ())), preferred_element_type=jnp.float32)


IN_TM = 256


def _in_proj_kernel(x_ref, gain_ref, w_ref, q_ref, kv_ref, pz_ref, guv_ref, g_ref):
    h = _rmsnorm(x_ref[...], gain_ref[...]).astype(jnp.bfloat16)
    o = 0
    for ref in (q_ref, kv_ref, pz_ref, guv_ref, g_ref):
        n = ref.shape[1]
        ref[...] = _dot(h, w_ref[:, o:o + n]).astype(ref.dtype)
        o += n


def _in_proj(x, gain, w_in):
    T = x.shape[0]
    widths = (Q_DIM, 2 * KV_DIM, POOL_DIM, 2 * GMLP_DIM, GATE_DIM)
    dtypes = (jnp.bfloat16, jnp.bfloat16, jnp.float32, jnp.float32, jnp.float32)
    return pl.pallas_call(
        _in_proj_kernel,
        grid=(T // IN_TM,),
        in_specs=[
            pl.BlockSpec((IN_TM, D_MODEL), lambda i: (i, 0)),
            pl.BlockSpec((1, D_MODEL), lambda i: (0, 0)),
            pl.BlockSpec(w_in.shape, lambda i: (0, 0)),
        ],
        out_specs=[pl.BlockSpec((IN_TM, n), lambda i: (i, 0)) for n in widths],
        out_shape=[jax.ShapeDtypeStruct((T, n), d) for n, d in zip(widths, dtypes)],
        compiler_params=pltpu.CompilerParams(
            dimension_semantics=("parallel",), vmem_limit_bytes=VMEM_LIMIT),
        name="in_proj",
    )(x, gain, w_in)


def _mixer_kernel(sinks_ref, x_ref, q_ref, kvc_ref, kvp_ref, pzc_ref, pzp_ref, guv_ref, g_ref,
                  bias_ref, wpool_ref, pscale_ref, vgain_ref, wsp_ref, bsp_ref,
                  wa_ref, wb_ref, wc_ref, wout_ref, bgate_ref, o_ref):
    i = pl.program_id(1)
    first = i == 0

    kvc = kvc_ref[...]
    kvp = kvp_ref[...]
    kk = jnp.concatenate([kvp[:, :KV_DIM], kvc[:, :KV_DIM]], axis=0)
    vv = jnp.concatenate([kvp[:, KV_DIM:], kvc[:, KV_DIM:]], axis=0)
    qi = lax.broadcasted_iota(jnp.int32, (BLK, 2 * BLK), 0)
    kj = lax.broadcasted_iota(jnp.int32, (BLK, 2 * BLK), 1)
    dist = qi + BLK - kj
    visible = (dist >= 0) & (dist < WINDOW) & jnp.logical_not(first & (kj < BLK))
    q = q_ref[...]
    heads = []
    for h in range(N_HEADS):
        g = h // GQA_GROUP
        qh = q[:, h * HEAD_DIM:(h + 1) * HEAD_DIM]
        kh = kk[:, g * HEAD_DIM:(g + 1) * HEAD_DIM]
        vh = vv[:, g * HEAD_DIM:(g + 1) * HEAD_DIM]
        s = _dot_nt(qh, kh) * (HEAD_DIM ** -0.5) + bias_ref[h]
        s = jnp.where(visible, s, NEG_INF)
        sink = sinks_ref[h]
        m = jnp.maximum(jnp.max(s, axis=-1, keepdims=True), sink)
        p = jnp.exp(s - m)
        denom = jnp.sum(p, axis=-1, keepdims=True) + jnp.exp(sink - m)
        heads.append(_dot(p.astype(jnp.bfloat16), vh) / denom)
    attn = jnp.concatenate(heads, axis=-1).astype(jnp.bfloat16)
    ya = _dot(attn, wa_ref[...])

    zc = pzc_ref[...]
    zp = jnp.where(first, 0.0, pzp_ref[...])
    ext = jnp.concatenate([zp, zc], axis=0)
    pos = i * BLK + lax.broadcasted_iota(jnp.int32, (BLK, 1), 0)
    pooled = []
    for gi, w in enumerate(POOL_WINDOWS):
        lo = gi * POOL_GROUP_DIM
        s = ext[:, lo:lo + POOL_GROUP_DIM]
        width = 1
        while width < w:
            s = s + pltpu.roll(s, width, 0)
            width *= 2
        count = jnp.minimum(pos + 1, w).astype(jnp.float32)
        d = s[POOL_TAIL:] / count - zc[:, lo:lo + POOL_GROUP_DIM]
        pooled.append(_dot(d.astype(jnp.bfloat16), wpool_ref[gi]))
    pool = (jnp.concatenate(pooled, axis=-1) * pscale_ref[...]).astype(jnp.bfloat16)
    yb = _dot(pool, wb_ref[...])

    guv = guv_ref[...]
    u = _gelu(guv[:, :GMLP_DIM])
    v = _rmsnorm(_gelu(guv[:, GMLP_DIM:]), vgain_ref[...]).astype(jnp.bfloat16)
    ti = lax.broadcasted_iota(jnp.int32, (BLK, BLK), 0)
    si = lax.broadcasted_iota(jnp.int32, (BLK, BLK), 1)
    bsp = bsp_ref[...]
    mixed = []
    gd = GMLP_DIM // GMLP_GROUPS
    for gi in range(GMLP_GROUPS):
        w = jnp.where(si <= ti, wsp_ref[gi], 0.0).astype(jnp.bfloat16)
        mixed.append(_dot(w, v[:, gi * gd:(gi + 1) * gd]) + bsp[:, gi:gi + 1])
    gm = (u * jnp.concatenate(mixed, axis=-1)).astype(jnp.bfloat16)
    yc = _dot(gm, wc_ref[...])

    gates = jax.nn.sigmoid(g_ref[...] + bgate_ref[...])
    merged = (gates[:, :D_MODEL] * ya + gates[:, D_MODEL:2 * D_MODEL] * yb
              + gates[:, 2 * D_MODEL:] * yc)
    o_ref[...] = x_ref[...] + _dot(merged.astype(jnp.bfloat16), wout_ref[...])


def _mixer(x, q, kv, pz, guv, g, sinks, bias, wpool, pscale, vgain, wsp, bsp,
           wa, wb, wc, wout, bgate, batch):
    T = x.shape[0]
    nb = T // batch // BLK
    tail_per_blk = BLK // POOL_TAIL

    def cur(b, i):
        return (b * nb + i, 0)

    def prev(b, i):
        return (b * nb + jnp.maximum(i - 1, 0), 0)

    def prev_tail(b, i):
        return (jnp.maximum((b * nb + i) * tail_per_blk - 1, 0), 0)

    def const(shape):
        return pl.BlockSpec(shape, lambda b, i: (0,) * len(shape))

    return pl.pallas_call(
        _mixer_kernel,
        grid=(batch, nb),
        in_specs=[
            pl.BlockSpec(memory_space=pltpu.SMEM),
            pl.BlockSpec((BLK, D_MODEL), cur),
            pl.BlockSpec((BLK, Q_DIM), cur),
            pl.BlockSpec((BLK, 2 * KV_DIM), cur),
            pl.BlockSpec((BLK, 2 * KV_DIM), prev),
            pl.BlockSpec((BLK, POOL_DIM), cur),
            pl.BlockSpec((POOL_TAIL, POOL_DIM), prev_tail),
            pl.BlockSpec((BLK, 2 * GMLP_DIM), cur),
            pl.BlockSpec((BLK, GATE_DIM), cur),
            const(bias.shape), const(wpool.shape), const(pscale.shape), const(vgain.shape),
            const(wsp.shape), const(bsp.shape), const(wa.shape), const(wb.shape),
            const(wc.shape), const(wout.shape), const(bgate.shape),
        ],
        out_specs=pl.BlockSpec((BLK, D_MODEL), cur),
        out_shape=jax.ShapeDtypeStruct((T, D_MODEL), jnp.float32),
        compiler_params=pltpu.CompilerParams(
            dimension_semantics=("parallel", "parallel"), vmem_limit_bytes=VMEM_LIMIT),
        name="mixer",
    )(sinks, x, q, kv, kv, pz, pz, guv, g, bias, wpool, pscale, vgain, wsp, bsp,
      wa, wb, wc, wout, bgate)


def _topk_rows(s, k):
    rows = s.shape[0]
    rid = lax.broadcasted_iota(jnp.int32, s.shape, 0)
    vals, ids = [], []
    for _ in range(k):
        m = jnp.max(s, axis=0, keepdims=True)
        sel = jnp.min(jnp.where(s == m, rid, rows), axis=0, keepdims=True)
        vals.append(m)
        ids.append(sel)
        s = jnp.where(rid == sel, -jnp.inf, s)
    return jnp.concatenate(vals, axis=0), jnp.concatenate(ids, axis=0)


def _pick_rows(table, which, rows):
    out = jnp.zeros(which.shape, table.dtype)
    for r in range(rows):
        out = jnp.where(which == r, table[r:r + 1], out)
    return out


_CAND_WIDTHS = tuple(PEER_TOPK // (a + 1) for a in range(PEER_TOPK))
_CAND_PAD = -sum(_CAND_WIDTHS) % SUBLANES


def _route_kernel(x_ref, gain_ref, wq_ref, k1_ref, k2_ref, h_ref, expert_ref, gate_ref):
    h = _rmsnorm(x_ref[...], gain_ref[...])
    h_ref[...] = h
    qt = _dot_nt(wq_ref[...], h.astype(jnp.bfloat16)).astype(jnp.bfloat16)
    experts = []
    for hd in range(PEER_HEADS):
        base = hd * PEER_QDIM
        s1 = _dot(k1_ref[...], qt[base:base + PEER_HALF])
        s2 = _dot(k2_ref[...], qt[base + PEER_HALF:base + PEER_QDIM])
        v1, i1 = _topk_rows(s1, PEER_TOPK)
        v2, i2 = _topk_rows(s2, PEER_TOPK)
        cand = jnp.concatenate(
            [v1[a:a + 1] + v2[:w] for a, w in enumerate(_CAND_WIDTHS)]
            + [jnp.full((_CAND_PAD, BLK), -jnp.inf, jnp.float32)], axis=0)
        cand_expert = jnp.concatenate(
            [i1[a:a + 1] * N_KEYS + i2[:w] for a, w in enumerate(_CAND_WIDTHS)]
            + [jnp.zeros((_CAND_PAD, BLK), jnp.int32)], axis=0)
        top, ci = _topk_rows(cand, PEER_TOPK)
        experts.append(_pick_rows(cand_expert, ci, sum(_CAND_WIDTHS)))
        ex = jnp.exp(top - top[0:1])
        gate_ref[0, pl.ds(hd * PEER_TOPK, PEER_TOPK), :] = ex / jnp.sum(ex, axis=0, keepdims=True)
    expert_ref[0] = jnp.concatenate(experts, axis=0).T


def _route(x, gain, wq_t, keys1, keys2):
    T = x.shape[0]
    nblk = T // BLK
    sel_spec = pl.BlockSpec((1, PEER_SEL, BLK), lambda i: (i, 0, 0))
    tok_spec = pl.BlockSpec((1, BLK, PEER_SEL), lambda i: (i, 0, 0))
    return pl.pallas_call(
        _route_kernel,
        grid=(nblk,),
        in_specs=[
            pl.BlockSpec((BLK, D_MODEL), lambda i: (i, 0)),
            pl.BlockSpec((1, D_MODEL), lambda i: (0, 0)),
            pl.BlockSpec(wq_t.shape, lambda i: (0, 0)),
            pl.BlockSpec(keys1.shape, lambda i: (0, 0)),
            pl.BlockSpec(keys2.shape, lambda i: (0, 0)),
        ],
        out_specs=[pl.BlockSpec((BLK, D_MODEL), lambda i: (i, 0)), tok_spec, sel_spec],
        out_shape=[
            jax.ShapeDtypeStruct((T, D_MODEL), jnp.float32),
            jax.ShapeDtypeStruct((nblk, BLK, PEER_SEL), jnp.int32),
            jax.ShapeDtypeStruct((nblk, PEER_SEL, BLK), jnp.float32),
        ],
        compiler_params=pltpu.CompilerParams(
            dimension_semantics=("parallel",), vmem_limit_bytes=VMEM_LIMIT),
        name="route",
    )(x, gain, wq_t, keys1, keys2)


def _row_pair(tab_ref, ea, eb):
    words = jnp.concatenate([tab_ref[ea], tab_ref[eb]], axis=0)
    lo = lax.bitcast_convert_type(lax.shift_left(words, 16), jnp.float32)
    hi = lax.bitcast_convert_type(words & jnp.int32(-65536), jnp.float32)
    return lo, hi


def _half_sums(q, sub):
    odd = (sub & 1) != 0
    c0 = jnp.where(odd, q[0] + pltpu.roll(q[0], 1, 0), q[1] + pltpu.roll(q[1], 7, 0))
    c1 = jnp.where(odd, q[2] + pltpu.roll(q[2], 1, 0), q[3] + pltpu.roll(q[3], 7, 0))
    return jnp.where((sub & 2) != 0, c0 + pltpu.roll(c0, 2, 0), c1 + pltpu.roll(c1, 6, 0))


_GROUP_PAIRS = ((3, 7), (2, 6), (1, 5), (0, 4))


N_GROUPS = PEER_SEL // SUBLANES
SLOT_LEN = BLK * N_GROUPS
V_CHUNK_GROUPS = 4


def _by_slot(a):
    nblk = a.shape[0]
    return (a.reshape(nblk, BLK, N_GROUPS, SUBLANES).transpose(0, 3, 1, 2)
            .reshape(nblk, SUBLANES, SLOT_LEN))


def _load_slots(src_hbm, dst_slots, sems, blk):
    copies = [pltpu.make_async_copy(src_hbm.at[blk, j], dst_slots[j], sems.at[j])
              for j in range(SUBLANES)]
    for copy in copies:
        copy.start()
    for copy in copies:
        copy.wait()


def _peer_u_kernel(expert_hbm, h_ref, gate_ref, tab_ref, coeff_ref, *scratch):
    expert_s, sems = scratch[:SUBLANES], scratch[SUBLANES]
    _load_slots(expert_hbm, expert_s, sems, pl.program_id(0))
    sub = lax.broadcasted_iota(jnp.int32, (SUBLANES, LANES), 0)
    lane = lax.broadcasted_iota(jnp.int32, (PEER_SEL, BLK), 1)

    def token(t, acc):
        ht = h_ref[t]
        h_lo = jnp.concatenate([ht[:HALF_ROWS], ht[:HALF_ROWS]], axis=0)
        h_hi = jnp.concatenate([ht[HALF_ROWS:], ht[HALF_ROWS:]], axis=0)
        tiles = []
        for grp in range(N_GROUPS):
            r = t * N_GROUPS + grp
            q = []
            for a, b in _GROUP_PAIRS:
                lo, hi = _row_pair(tab_ref, expert_s[a][r], expert_s[b][r])
                q.append(lo * h_lo + hi * h_hi)
            tiles.append(_half_sums(q, sub))
        act = jnp.sum(jnp.concatenate(tiles, axis=0), axis=1, keepdims=True)
        return jnp.where(lane == t, act, acc)

    act = lax.fori_loop(0, BLK, token, jnp.zeros((PEER_SEL, BLK), jnp.float32))
    coeff_ref[0] = (gate_ref[0] * _gelu(act)).T


def _peer_v_kernel(expert_hbm, coeff_hbm, x_ref, tab_ref, o_ref, *scratch):
    expert_s, coeff_s = scratch[:SUBLANES], scratch[SUBLANES:2 * SUBLANES]
    expert_sems, coeff_sems = scratch[2 * SUBLANES:]
    blk = pl.program_id(0)
    _load_slots(expert_hbm, expert_s, expert_sems, blk)
    _load_slots(coeff_hbm, coeff_s, coeff_sems, blk)
    first_half = lax.broadcasted_iota(jnp.int32, (SUBLANES, LANES), 0) < HALF_ROWS
    o_ref[...] = x_ref[...]
    chunks = N_GROUPS // V_CHUNK_GROUPS

    def chunk(n, carry):
        acc_lo = [jnp.zeros((SUBLANES, LANES), jnp.float32) for _ in range(2)]
        acc_hi = [jnp.zeros((SUBLANES, LANES), jnp.float32) for _ in range(2)]
        for grp in range(V_CHUNK_GROUPS):
            r = n * V_CHUNK_GROUPS + grp
            for p in range(SUBLANES // 2):
                lo, hi = _row_pair(tab_ref, expert_s[2 * p][r], expert_s[2 * p + 1][r])
                c = jnp.where(first_half, coeff_s[2 * p][r], coeff_s[2 * p + 1][r])
                acc_lo[p % 2] = acc_lo[p % 2] + c * lo
                acc_hi[p % 2] = acc_hi[p % 2] + c * hi
        y_lo = acc_lo[0] + acc_lo[1]
        y_hi = acc_hi[0] + acc_hi[1]
        y = jnp.concatenate([y_lo[:HALF_ROWS] + y_lo[HALF_ROWS:],
                             y_hi[:HALF_ROWS] + y_hi[HALF_ROWS:]], axis=0)
        t = n // chunks
        o_ref[t] = o_ref[t] + y
        return carry

    lax.fori_loop(0, BLK * chunks, chunk, 0)


def _table_spec(tab):
    return pl.BlockSpec(tab.shape, lambda i: (0, 0, 0), pipeline_mode=pl.Buffered(1))


def _slot_scratch(dtype):
    return [pltpu.SMEM((SLOT_LEN,), dtype) for _ in range(SUBLANES)]


def _peer_u(expert, h3, gate, tab):
    nblk = expert.shape[0]
    return pl.pallas_call(
        _peer_u_kernel,
        grid=(nblk,),
        in_specs=[
            pl.BlockSpec(memory_space=pl.ANY),
            pl.BlockSpec((BLK, SUBLANES, LANES), lambda i: (i, 0, 0)),
            pl.BlockSpec((1, PEER_SEL, BLK), lambda i: (i, 0, 0)),
            _table_spec(tab),
        ],
        out_specs=pl.BlockSpec((1, BLK, PEER_SEL), lambda i: (i, 0, 0)),
        out_shape=jax.ShapeDtypeStruct((nblk, BLK, PEER_SEL), jnp.float32),
        scratch_shapes=_slot_scratch(jnp.int32) + [pltpu.SemaphoreType.DMA((SUBLANES,))],
        compiler_params=pltpu.CompilerParams(
            dimension_semantics=("arbitrary",), vmem_limit_bytes=VMEM_LIMIT),
        name="peer_u",
    )(expert, h3, gate, tab)


def _peer_v(expert, coeff, x3, tab):
    nblk = expert.shape[0]
    tok_spec = pl.BlockSpec((BLK, SUBLANES, LANES), lambda i: (i, 0, 0))
    return pl.pallas_call(
        _peer_v_kernel,
        grid=(nblk,),
        in_specs=[
            pl.BlockSpec(memory_space=pl.ANY),
            pl.BlockSpec(memory_space=pl.ANY),
            tok_spec,
            _table_spec(tab),
        ],
        out_specs=tok_spec,
        out_shape=jax.ShapeDtypeStruct(x3.shape, jnp.float32),
        scratch_shapes=(_slot_scratch(jnp.int32) + _slot_scratch(jnp.float32)
                        + [pltpu.SemaphoreType.DMA((SUBLANES,)), pltpu.SemaphoreType.DMA((SUBLANES,))]),
        compiler_params=pltpu.CompilerParams(
            dimension_semantics=("arbitrary",), vmem_limit_bytes=VMEM_LIMIT),
        name="peer_v",
    )(expert, coeff, x3, tab)


NORM_TM = 512


def _final_norm_kernel(x_ref, gain_ref, o_ref):
    o_ref[...] = _rmsnorm(x_ref[...], gain_ref[...])


def _final_norm(x, gain):
    T = x.shape[0]
    return pl.pallas_call(
        _final_norm_kernel,
        grid=(T // NORM_TM,),
        in_specs=[pl.BlockSpec((NORM_TM, D_MODEL), lambda i: (i, 0)),
                  pl.BlockSpec((1, D_MODEL), lambda i: (0, 0))],
        out_specs=pl.BlockSpec((NORM_TM, D_MODEL), lambda i: (i, 0)),
        out_shape=jax.ShapeDtypeStruct((T, D_MODEL), jnp.float32),
        compiler_params=pltpu.CompilerParams(dimension_semantics=("parallel",)),
        name="final_norm",
    )(x, gain)


def _t5_causal_bucket(dist):
    n = jnp.maximum(dist, 0)
    max_exact = N_BUCKETS // 2
    nf = jnp.maximum(n, 1).astype(jnp.float32)
    large = max_exact + (jnp.log(nf / max_exact) / math.log(MAX_DISTANCE / max_exact)
                         * (N_BUCKETS - max_exact)).astype(jnp.int32)
    large = jnp.minimum(large, N_BUCKETS - 1)
    return jnp.where(n < max_exact, n, large)


def _attention_bias(rel_bias):
    qi = jnp.arange(BLK)[:, None]
    kj = jnp.arange(2 * BLK)[None, :]
    bias = rel_bias[_t5_causal_bucket(qi + BLK - kj)].astype(jnp.float32)
    return bias.transpose(2, 0, 1)


def _pack_table(tab):
    bits = lax.bitcast_convert_type(tab.astype(jnp.bfloat16), jnp.uint16).astype(jnp.uint32)
    words = bits[:, :D_MODEL // 2] | (bits[:, D_MODEL // 2:] << 16)
    return lax.bitcast_convert_type(words, jnp.int32).reshape(N_EXPERTS, HALF_ROWS, LANES)


def kernel(x, rel_bias, final_gain, attn_norm_gain, w_in, b_gate, attn_sinks, w_pool, pool_scale, gmlp_v_gain, w_spatial, b_spatial, w_branch_a, w_branch_b, w_branch_c, w_out, ffn_norm_gain, peer_w_query, peer_sub_keys1, peer_sub_keys2, peer_expert_u, peer_expert_v):
    B, S, D = x.shape
    T = B * S
    depth = w_in.shape[0]
    bf16 = jnp.bfloat16
    bias = _attention_bias(rel_bias)
    xt = x.reshape(T, D)
    for l in range(depth):
        q, kv, pz, guv, g = _in_proj(xt, attn_norm_gain[l][None], w_in[l].astype(bf16))
        xt = _mixer(xt, q, kv, pz, guv, g, attn_sinks[l], bias,
                    w_pool[l].astype(bf16), pool_scale[l][None], gmlp_v_gain[l][None],
                    w_spatial[l], b_spatial[l].T,
                    w_branch_a[l].astype(bf16), w_branch_b[l].astype(bf16),
                    w_branch_c[l].astype(bf16), w_out[l].astype(bf16), b_gate[l][None], B)
        h, expert, gate = _route(xt, ffn_norm_gain[l][None], peer_w_query[l].T.astype(bf16),
                                 peer_sub_keys1[l].astype(bf16), peer_sub_keys2[l].astype(bf16))
        expert = _by_slot(expert)
        coeff = _peer_u(expert, h.reshape(T, SUBLANES, LANES), gate, _pack_table(peer_expert_u[l]))
        xt = _peer_v(expert, _by_slot(coeff), xt.reshape(T, SUBLANES, LANES),
                     _pack_table(peer_expert_v[l])).reshape(T, D)
    return _final_norm(xt, final_gain[None]).reshape(B, S, D)
```

```python
import functools
import math

import jax
import jax.numpy as jnp
from jax import lax
from jax.experimental import pallas as pl
from jax.experimental.pallas import tpu as pltpu

D_MODEL = 1024
N_HEADS = 8
N_KV_HEADS = 2
HEAD_DIM = 64
GQA_GROUP = N_HEADS // N_KV_HEADS
WINDOW = 128
BLK = 128
Q_DIM = N_HEADS * HEAD_DIM
KV_DIM = N_KV_HEADS * HEAD_DIM
N_BUCKETS = 32
MAX_DISTANCE = 128
POOL_WINDOWS = (2, 4, 8, 16)
POOL_GROUP_DIM = 128
POOL_DIM = 512
POOL_TAIL = 16
GMLP_DIM = 512
GMLP_GROUPS = 4
GATE_DIM = 3 * D_MODEL
PEER_HEADS = 8
N_KEYS = 128
PEER_TOPK = 16
PEER_QDIM = 256
PEER_HALF = 128
PEER_SEL = PEER_HEADS * PEER_TOPK
EPS = 1e-6
NEG_INF = -1e30

SUBLANES = 8
LANES = 128
N_EXPERTS = N_KEYS * N_KEYS
HALF_ROWS = SUBLANES // 2
VMEM_LIMIT = 56 * 1024 * 1024


def _rmsnorm(x, gain):
    return x * lax.rsqrt(jnp.mean(x * x, axis=-1, keepdims=True) + EPS) * gain


def _gelu(x):
    c = math.sqrt(2.0 / math.pi)
    return x * (0.5 * (1.0 + jnp.tanh(c * (x + 0.044715 * (x * x * x)))))


def _dot(a, b):
    return jnp.dot(a, b, preferred_element_type=jnp.float32)


def _dot_nt(a, b):
    return lax.dot_general(a, b, (((1,), (1,)), ((), ())), preferred_element_type=jnp.float32)


IN_TM = 256


def _in_proj_kernel(x_ref, gain_ref, w_ref, q_ref, kv_ref, pz_ref, guv_ref, g_ref):
    h = _rmsnorm(x_ref[...], gain_ref[...]).astype(jnp.bfloat16)
    o = 0
    for ref in (q_ref, kv_ref, pz_ref, guv_ref, g_ref):
        n = ref.shape[1]
        ref[...] = _dot(h, w_ref[:, o:o + n]).astype(ref.dtype)
        o += n


def _in_proj(x, gain, w_in):
    T = x.shape[0]
    widths = (Q_DIM, 2 * KV_DIM, POOL_DIM, 2 * GMLP_DIM, GATE_DIM)
    dtypes = (jnp.bfloat16, jnp.bfloat16, jnp.float32, jnp.float32, jnp.float32)
    return pl.pallas_call(
        _in_proj_kernel,
        grid=(T // IN_TM,),
        in_specs=[
            pl.BlockSpec((IN_TM, D_MODEL), lambda i: (i, 0)),
            pl.BlockSpec((1, D_MODEL), lambda i: (0, 0)),
            pl.BlockSpec(w_in.shape, lambda i: (0, 0)),
        ],
        out_specs=[pl.BlockSpec((IN_TM, n), lambda i: (i, 0)) for n in widths],
        out_shape=[jax.ShapeDtypeStruct((T, n), d) for n, d in zip(widths, dtypes)],
        compiler_params=pltpu.CompilerParams(
            dimension_semantics=("parallel",), vmem_limit_bytes=VMEM_LIMIT),
        name="in_proj",
    )(x, gain, w_in)


def _mixer_kernel(sinks_ref, x_ref, q_ref, kvc_ref, kvp_ref, pzc_ref, pzp_ref, guv_ref, g_ref,
                  bias_ref, wpool_ref, pscale_ref, vgain_ref, wsp_ref, bsp_ref,
                  wa_ref, wb_ref, wc_ref, wout_ref, bgate_ref, o_ref):
    i = pl.program_id(1)
    first = i == 0

    kvc = kvc_ref[...]
    kvp = kvp_ref[...]
    kk = jnp.concatenate([kvp[:, :KV_DIM], kvc[:, :KV_DIM]], axis=0)
    vv = jnp.concatenate([kvp[:, KV_DIM:], kvc[:, KV_DIM:]], axis=0)
    qi = lax.broadcasted_iota(jnp.int32, (BLK, 2 * BLK), 0)
    kj = lax.broadcasted_iota(jnp.int32, (BLK, 2 * BLK), 1)
    dist = qi + BLK - kj
    visible = (dist >= 0) & (dist < WINDOW) & jnp.logical_not(first & (kj < BLK))
    q = q_ref[...]
    heads = []
    for h in range(N_HEADS):
        g = h // GQA_GROUP
        qh = q[:, h * HEAD_DIM:(h + 1) * HEAD_DIM]
        kh = kk[:, g * HEAD_DIM:(g + 1) * HEAD_DIM]
        vh = vv[:, g * HEAD_DIM:(g + 1) * HEAD_DIM]
        s = _dot_nt(qh, kh) * (HEAD_DIM ** -0.5) + bias_ref[h]
        s = jnp.where(visible, s, NEG_INF)
        sink = sinks_ref[h]
        m = jnp.maximum(jnp.max(s, axis=-1, keepdims=True), sink)
        p = jnp.exp(s - m)
        denom = jnp.sum(p, axis=-1, keepdims=True) + jnp.exp(sink - m)
        heads.append(_dot(p.astype(jnp.bfloat16), vh) / denom)
    attn = jnp.concatenate(heads, axis=-1).astype(jnp.bfloat16)
    ya = _dot(attn, wa_ref[...])

    zc = pzc_ref[...]
    zp = jnp.where(first, 0.0, pzp_ref[...])
    ext = jnp.concatenate([zp, zc], axis=0)
    pos = i * BLK + lax.broadcasted_iota(jnp.int32, (BLK, 1), 0)
    pooled = []
    for gi, w in enumerate(POOL_WINDOWS):
        lo = gi * POOL_GROUP_DIM
        s = ext[:, lo:lo + POOL_GROUP_DIM]
        width = 1
        while width < w:
            s = s + pltpu.roll(s, width, 0)
            width *= 2
        count = jnp.minimum(pos + 1, w).astype(jnp.float32)
        d = s[POOL_TAIL:] / count - zc[:, lo:lo + POOL_GROUP_DIM]
        pooled.append(_dot(d.astype(jnp.bfloat16), wpool_ref[gi]))
    pool = (jnp.concatenate(pooled, axis=-1) * pscale_ref[...]).astype(jnp.bfloat16)
    yb = _dot(pool, wb_ref[...])

    guv = guv_ref[...]
    u = _gelu(guv[:, :GMLP_DIM])
    v = _rmsnorm(_gelu(guv[:, GMLP_DIM:]), vgain_ref[...]).astype(jnp.bfloat16)
    ti = lax.broadcasted_iota(jnp.int32, (BLK, BLK), 0)
    si = lax.broadcasted_iota(jnp.int32, (BLK, BLK), 1)
    bsp = bsp_ref[...]
    mixed = []
    gd = GMLP_DIM // GMLP_GROUPS
    for gi in range(GMLP_GROUPS):
        w = jnp.where(si <= ti, wsp_ref[gi], 0.0).astype(jnp.bfloat16)
        mixed.append(_dot(w, v[:, gi * gd:(gi + 1) * gd]) + bsp[:, gi:gi + 1])
    gm = (u * jnp.concatenate(mixed, axis=-1)).astype(jnp.bfloat16)
    yc = _dot(gm, wc_ref[...])

    gates = jax.nn.sigmoid(g_ref[...] + bgate_ref[...])
    merged = (gates[:, :D_MODEL] * ya + gates[:, D_MODEL:2 * D_MODEL] * yb
              + gates[:, 2 * D_MODEL:] * yc)
    o_ref[...] = x_ref[...] + _dot(merged.astype(jnp.bfloat16), wout_ref[...])


def _mixer(x, q, kv, pz, guv, g, sinks, bias, wpool, pscale, vgain, wsp, bsp,
           wa, wb, wc, wout, bgate, batch):
    T = x.shape[0]
    nb = T // batch // BLK
    tail_per_blk = BLK // POOL_TAIL

    def cur(b, i):
        return (b * nb + i, 0)

    def prev(b, i):
        return (b * nb + jnp.maximum(i - 1, 0), 0)

    def prev_tail(b, i):
        return (jnp.maximum((b * nb + i) * tail_per_blk - 1, 0), 0)

    def const(shape):
        return pl.BlockSpec(shape, lambda b, i: (0,) * len(shape))

    return pl.pallas_call(
        _mixer_kernel,
        grid=(batch, nb),
        in_specs=[
            pl.BlockSpec(memory_space=pltpu.SMEM),
            pl.BlockSpec((BLK, D_MODEL), cur),
            pl.BlockSpec((BLK, Q_DIM), cur),
            pl.BlockSpec((BLK, 2 * KV_DIM), cur),
            pl.BlockSpec((BLK, 2 * KV_DIM), prev),
            pl.BlockSpec((BLK, POOL_DIM), cur),
            pl.BlockSpec((POOL_TAIL, POOL_DIM), prev_tail),
            pl.BlockSpec((BLK, 2 * GMLP_DIM), cur),
            pl.BlockSpec((BLK, GATE_DIM), cur),
            const(bias.shape), const(wpool.shape), const(pscale.shape), const(vgain.shape),
            const(wsp.shape), const(bsp.shape), const(wa.shape), const(wb.shape),
            const(wc.shape), const(wout.shape), const(bgate.shape),
        ],
        out_specs=pl.BlockSpec((BLK, D_MODEL), cur),
        out_shape=jax.ShapeDtypeStruct((T, D_MODEL), jnp.float32),
        compiler_params=pltpu.CompilerParams(
            dimension_semantics=("parallel", "parallel"), vmem_limit_bytes=VMEM_LIMIT),
        name="mixer",
    )(sinks, x, q, kv, kv, pz, pz, guv, g, bias, wpool, pscale, vgain, wsp, bsp,
      wa, wb, wc, wout, bgate)


def _topk_rows(s, k):
    rows = s.shape[0]
    rid = lax.broadcasted_iota(jnp.int32, s.shape, 0)
    vals, ids = [], []
    for _ in range(k):
        m = jnp.max(s, axis=0, keepdims=True)
        sel = jnp.min(jnp.where(s == m, rid, rows), axis=0, keepdims=True)
        vals.append(m)
        ids.append(sel)
        s = jnp.where(rid == sel, -jnp.inf, s)
    return jnp.concatenate(vals, axis=0), jnp.concatenate(ids, axis=0)


def _pick_rows(table, which, rows):
    out = jnp.zeros(which.shape, table.dtype)
    for r in range(rows):
        out = jnp.where(which == r, table[r:r + 1], out)
    return out


_CAND_WIDTHS = tuple(PEER_TOPK // (a + 1) for a in range(PEER_TOPK))
_CAND_PAD = -sum(_CAND_WIDTHS) % SUBLANES


def _route_kernel(x_ref, gain_ref, wq_ref, k1_ref, k2_ref, h_ref, expert_ref, gate_ref):
    h = _rmsnorm(x_ref[...], gain_ref[...])
    h_ref[...] = h
    qt = _dot_nt(wq_ref[...], h.astype(jnp.bfloat16)).astype(jnp.bfloat16)
    experts = []
    for hd in range(PEER_HEADS):
        base = hd * PEER_QDIM
        s1 = _dot(k1_ref[...], qt[base:base + PEER_HALF])
        s2 = _dot(k2_ref[...], qt[base + PEER_HALF:base + PEER_QDIM])
        v1, i1 = _topk_rows(s1, PEER_TOPK)
        v2, i2 = _topk_rows(s2, PEER_TOPK)
        cand = jnp.concatenate(
            [v1[a:a + 1] + v2[:w] for a, w in enumerate(_CAND_WIDTHS)]
            + [jnp.full((_CAND_PAD, BLK), -jnp.inf, jnp.float32)], axis=0)
        cand_expert = jnp.concatenate(
            [i1[a:a + 1] * N_KEYS + i2[:w] for a, w in enumerate(_CAND_WIDTHS)]
            + [jnp.zeros((_CAND_PAD, BLK), jnp.int32)], axis=0)
        top, ci = _topk_rows(cand, PEER_TOPK)
        experts.append(_pick_rows(cand_expert, ci, sum(_CAND_WIDTHS)))
        ex = jnp.exp(top - top[0:1])
        gate_ref[0, pl.ds(hd * PEER_TOPK, PEER_TOPK), :] = ex / jnp.sum(ex, axis=0, keepdims=True)
    expert_ref[0] = jnp.concatenate(experts, axis=0).T


def _route(x, gain, wq_t, keys1, keys2):
    T = x.shape[0]
    nblk = T // BLK
    sel_spec = pl.BlockSpec((1, PEER_SEL, BLK), lambda i: (i, 0, 0))
    tok_spec = pl.BlockSpec((1, BLK, PEER_SEL), lambda i: (i, 0, 0))
    return pl.pallas_call(
        _route_kernel,
        grid=(nblk,),
        in_specs=[
            pl.BlockSpec((BLK, D_MODEL), lambda i: (i, 0)),
            pl.BlockSpec((1, D_MODEL), lambda i: (0, 0)),
            pl.BlockSpec(wq_t.shape, lambda i: (0, 0)),
            pl.BlockSpec(keys1.shape, lambda i: (0, 0)),
            pl.BlockSpec(keys2.shape, lambda i: (0, 0)),
        ],
        out_specs=[pl.BlockSpec((BLK, D_MODEL), lambda i: (i, 0)), tok_spec, sel_spec],
        out_shape=[
            jax.ShapeDtypeStruct((T, D_MODEL), jnp.float32),
            jax.ShapeDtypeStruct((nblk, BLK, PEER_SEL), jnp.int32),
            jax.ShapeDtypeStruct((nblk, PEER_SEL, BLK), jnp.float32),
        ],
        compiler_params=pltpu.CompilerParams(
            dimension_semantics=("parallel",), vmem_limit_bytes=VMEM_LIMIT),
        name="route",
    )(x, gain, wq_t, keys1, keys2)


def _row_pair(tab_ref, ea, eb):
    words = jnp.concatenate([tab_ref[ea], tab_ref[eb]], axis=0)
    lo = lax.bitcast_convert_type(lax.shift_left(words, 16), jnp.float32)
    hi = lax.bitcast_convert_type(words & jnp.int32(-65536), jnp.float32)
    return lo, hi


def _half_sums(q, sub):
    odd = (sub & 1) != 0
    c0 = jnp.where(odd, q[0] + pltpu.roll(q[0], 1, 0), q[1] + pltpu.roll(q[1], 7, 0))
    c1 = jnp.where(odd, q[2] + pltpu.roll(q[2], 1, 0), q[3] + pltpu.roll(q[3], 7, 0))
    return jnp.where((sub & 2) != 0, c0 + pltpu.roll(c0, 2, 0), c1 + pltpu.roll(c1, 6, 0))


_GROUP_PAIRS = ((3, 7), (2, 6), (1, 5), (0, 4))


N_GROUPS = PEER_SEL // SUBLANES
SLOT_LEN = BLK * N_GROUPS
U_CHUNK_SHIFT = 1
U_CHUNK_GROUPS = N_GROUPS >> U_CHUNK_SHIFT
V_CHUNK_SHIFT = 1
V_CHUNK_GROUPS = N_GROUPS >> V_CHUNK_SHIFT


def _by_slot(a):
    nblk = a.shape[0]
    return (a.reshape(nblk, BLK, N_GROUPS, SUBLANES).transpose(0, 3, 1, 2)
            .reshape(nblk, SUBLANES, SLOT_LEN))


def _slot_copies(src_hbm, dst_slots, sems, blk, half):
    return [pltpu.make_async_copy(src_hbm.at[blk, j],
                                  dst_slots[j].at[pl.ds(half * SLOT_LEN, SLOT_LEN)],
                                  sems.at[half, j])
            for j in range(SUBLANES)]


def _fetch_slots(sources):
    i = pl.program_id(0)
    half = i % 2

    @pl.when(i == 0)
    def _():
        for src, dst, sems in sources:
            for copy in _slot_copies(src, dst, sems, 0, 0):
                copy.start()

    for src, dst, sems in sources:
        for copy in _slot_copies(src, dst, sems, i, half):
            copy.wait()

    @pl.when(i + 1 < pl.num_programs(0))
    def _():
        for src, dst, sems in sources:
            for copy in _slot_copies(src, dst, sems, i + 1, 1 - half):
                copy.start()

    return half * SLOT_LEN


def _peer_u_kernel(expert_hbm, h_ref, gate_ref, tab_ref, coeff_ref, *scratch):
    expert_s = scratch[:SUBLANES]
    sems, tiles_ref, act_ref = scratch[SUBLANES:]
    base = _fetch_slots([(expert_hbm, expert_s, sems)])
    sub = lax.broadcasted_iota(jnp.int32, (SUBLANES, LANES), 0)
    chunks = N_GROUPS // U_CHUNK_GROUPS
    rows = U_CHUNK_GROUPS * SUBLANES
    lane = lax.broadcasted_iota(jnp.int32, (rows, BLK), 1)
    tiles_ref[...] = jnp.zeros(tiles_ref.shape, jnp.float32)
    act_ref[...] = jnp.zeros(act_ref.shape, jnp.float32)

    def reduce_trip(m):
        src = pl.ds(pl.multiple_of((m & 1) * rows, rows), rows)
        dst = pl.ds(pl.multiple_of((m & (chunks - 1)) * rows, rows), rows)
        s = jnp.sum(tiles_ref[src, :], axis=1, keepdims=True)
        act_ref[dst, :] = jnp.where(lane == (m >> U_CHUNK_SHIFT), s, act_ref[dst, :])

    def chunk(n, carry):
        reduce_trip(n - 1)
        ht = h_ref[n >> U_CHUNK_SHIFT]
        h_lo = jnp.concatenate([ht[:HALF_ROWS], ht[:HALF_ROWS]], axis=0)
        h_hi = jnp.concatenate([ht[HALF_ROWS:], ht[HALF_ROWS:]], axis=0)
        for grp in range(U_CHUNK_GROUPS):
            r = base + n * U_CHUNK_GROUPS + grp
            q = []
            for a, b in _GROUP_PAIRS:
                lo, hi = _row_pair(tab_ref, expert_s[a][r], expert_s[b][r])
                q.append(lo * h_lo + hi * h_hi)
            dst = (n & 1) * rows + grp * SUBLANES
            tiles_ref[pl.ds(pl.multiple_of(dst, SUBLANES), SUBLANES), :] = _half_sums(q, sub)
        return carry

    lax.fori_loop(0, BLK * chunks, chunk, 0)
    reduce_trip(BLK * chunks - 1)
    coeff_ref[0] = (gate_ref[0] * _gelu(act_ref[...])).T


def _peer_v_kernel(expert_hbm, coeff_hbm, x_ref, tab_ref, o_ref, *scratch):
    expert_s, coeff_s = scratch[:SUBLANES], scratch[SUBLANES:2 * SUBLANES]
    expert_sems, coeff_sems = scratch[2 * SUBLANES:]
    base = _fetch_slots([(expert_hbm, expert_s, expert_sems), (coeff_hbm, coeff_s, coeff_sems)])
    first_half = lax.broadcasted_iota(jnp.int32, (SUBLANES, LANES), 0) < HALF_ROWS
    o_ref[...] = x_ref[...]
    chunks = N_GROUPS // V_CHUNK_GROUPS

    def chunk(n, carry):
        acc_lo = [jnp.zeros((SUBLANES, LANES), jnp.float32) for _ in range(2)]
        acc_hi = [jnp.zeros((SUBLANES, LANES), jnp.float32) for _ in range(2)]
        for grp in range(V_CHUNK_GROUPS):
            r = base + n * V_CHUNK_GROUPS + grp
            for p in range(SUBLANES // 2):
                lo, hi = _row_pair(tab_ref, expert_s[2 * p][r], expert_s[2 * p + 1][r])
                c = jnp.where(first_half, coeff_s[2 * p][r], coeff_s[2 * p + 1][r])
                acc_lo[p % 2] = acc_lo[p % 2] + c * lo
                acc_hi[p % 2] = acc_hi[p % 2] + c * hi
        y_lo = acc_lo[0] + acc_lo[1]
        y_hi = acc_hi[0] + acc_hi[1]
        y = jnp.concatenate([y_lo[:HALF_ROWS] + y_lo[HALF_ROWS:],
                             y_hi[:HALF_ROWS] + y_hi[HALF_ROWS:]], axis=0)
        t = n >> V_CHUNK_SHIFT
        o_ref[t] = o_ref[t] + y
        return carry

    lax.fori_loop(0, BLK * chunks, chunk, 0)


def _table_spec(tab):
    return pl.BlockSpec(tab.shape, lambda i: (0, 0, 0), pipeline_mode=pl.Buffered(1))


def _slot_scratch(dtype):
    return [pltpu.SMEM((2 * SLOT_LEN,), dtype) for _ in range(SUBLANES)]


def _slot_sems():
    return pltpu.SemaphoreType.DMA((2, SUBLANES))


def _peer_u(expert, h3, gate, tab):
    nblk = expert.shape[0]
    return pl.pallas_call(
        _peer_u_kernel,
        grid=(nblk,),
        in_specs=[
            pl.BlockSpec(memory_space=pl.ANY),
            pl.BlockSpec((BLK, SUBLANES, LANES), lambda i: (i, 0, 0)),
            pl.BlockSpec((1, PEER_SEL, BLK), lambda i: (i, 0, 0)),
            _table_spec(tab),
        ],
        out_specs=pl.BlockSpec((1, BLK, PEER_SEL), lambda i: (i, 0, 0)),
        out_shape=jax.ShapeDtypeStruct((nblk, BLK, PEER_SEL), jnp.float32),
        scratch_shapes=_slot_scratch(jnp.int32) + [
            _slot_sems(),
            pltpu.VMEM((2 * U_CHUNK_GROUPS * SUBLANES, LANES), jnp.float32),
            pltpu.VMEM((PEER_SEL, BLK), jnp.float32),
        ],
        compiler_params=pltpu.CompilerParams(
            dimension_semantics=("arbitrary",), vmem_limit_bytes=VMEM_LIMIT),
        name="peer_u",
    )(expert, h3, gate, tab)


def _peer_v(expert, coeff, x3, tab):
    nblk = expert.shape[0]
    tok_spec = pl.BlockSpec((BLK, SUBLANES, LANES), lambda i: (i, 0, 0))
    return pl.pallas_call(
        _peer_v_kernel,
        grid=(nblk,),
        in_specs=[
            pl.BlockSpec(memory_space=pl.ANY),
            pl.BlockSpec(memory_space=pl.ANY),
            tok_spec,
            _table_spec(tab),
        ],
        out_specs=tok_spec,
        out_shape=jax.ShapeDtypeStruct(x3.shape, jnp.float32),
        scratch_shapes=(_slot_scratch(jnp.int32) + _slot_scratch(jnp.float32)
                        + [_slot_sems(), _slot_sems()]),
        compiler_params=pltpu.CompilerParams(
            dimension_semantics=("arbitrary",), vmem_limit_bytes=VMEM_LIMIT),
        name="peer_v",
    )(expert, coeff, x3, tab)


NORM_TM = 512


def _final_norm_kernel(x_ref, gain_ref, o_ref):
    o_ref[...] = _rmsnorm(x_ref[...], gain_ref[...])


def _final_norm(x, gain):
    T = x.shape[0]
    return pl.pallas_call(
        _final_norm_kernel,
        grid=(T // NORM_TM,),
        in_specs=[pl.BlockSpec((NORM_TM, D_MODEL), lambda i: (i, 0)),
                  pl.BlockSpec((1, D_MODEL), lambda i: (0, 0))],
        out_specs=pl.BlockSpec((NORM_TM, D_MODEL), lambda i: (i, 0)),
        out_shape=jax.ShapeDtypeStruct((T, D_MODEL), jnp.float32),
        compiler_params=pltpu.CompilerParams(dimension_semantics=("parallel",)),
        name="final_norm",
    )(x, gain)


def _t5_causal_bucket(dist):
    n = jnp.maximum(dist, 0)
    max_exact = N_BUCKETS // 2
    nf = jnp.maximum(n, 1).astype(jnp.float32)
    large = max_exact + (jnp.log(nf / max_exact) / math.log(MAX_DISTANCE / max_exact)
                         * (N_BUCKETS - max_exact)).astype(jnp.int32)
    large = jnp.minimum(large, N_BUCKETS - 1)
    return jnp.where(n < max_exact, n, large)


def _attention_bias(rel_bias):
    qi = jnp.arange(BLK)[:, None]
    kj = jnp.arange(2 * BLK)[None, :]
    bias = rel_bias[_t5_causal_bucket(qi + BLK - kj)].astype(jnp.float32)
    return bias.transpose(2, 0, 1)


def _pack_table(tab):
    bits = lax.bitcast_convert_type(tab.astype(jnp.bfloat16), jnp.uint16).astype(jnp.uint32)
    words = bits[:, :D_MODEL // 2] | (bits[:, D_MODEL // 2:] << 16)
    return lax.bitcast_convert_type(words, jnp.int32).reshape(N_EXPERTS, HALF_ROWS, LANES)


def kernel(x, rel_bias, final_gain, attn_norm_gain, w_in, b_gate, attn_sinks, w_pool, pool_scale, gmlp_v_gain, w_spatial, b_spatial, w_branch_a, w_branch_b, w_branch_c, w_out, ffn_norm_gain, peer_w_query, peer_sub_keys1, peer_sub_keys2, peer_expert_u, peer_expert_v):
    B, S, D = x.shape
    T = B * S
    depth = w_in.shape[0]
    bf16 = jnp.bfloat16
    bias = _attention_bias(rel_bias)
    xt = x.reshape(T, D)
    for l in range(depth):
        q, kv, pz, guv, g = _in_proj(xt, attn_norm_gain[l][None], w_in[l].astype(bf16))
        xt = _mixer(xt, q, kv, pz, guv, g, attn_sinks[l], bias,
                    w_pool[l].astype(bf16), pool_scale[l][None], gmlp_v_gain[l][None],
                    w_spatial[l], b_spatial[l].T,
                    w_branch_a[l].astype(bf16), w_branch_b[l].astype(bf16),
                    w_branch_c[l].astype(bf16), w_out[l].astype(bf16), b_gate[l][None], B)
        h, expert, gate = _route(xt, ffn_norm_gain[l][None], peer_w_query[l].T.astype(bf16),
                                 peer_sub_keys1[l].astype(bf16), peer_sub_keys2[l].astype(bf16))
        expert = _by_slot(expert)
        coeff = _peer_u(expert, h.reshape(T, SUBLANES, LANES), gate, _pack_table(peer_expert_u[l]))
        xt = _peer_v(expert, _by_slot(coeff), xt.reshape(T, SUBLANES, LANES),
                     _pack_table(peer_expert_v[l])).reshape(T, D)
    return _final_norm(xt, final_gain[None]).reshape(B, S, D)
```

```python
import functools
import math

import jax
import jax.numpy as jnp
from jax import lax
from jax.experimental import pallas as pl
from jax.experimental.pallas import tpu as pltpu

D_MODEL = 1024
N_HEADS = 8
N_KV_HEADS = 2
HEAD_DIM = 64
GQA_GROUP = N_HEADS // N_KV_HEADS
WINDOW = 128
BLK = 128
Q_DIM = N_HEADS * HEAD_DIM
KV_DIM = N_KV_HEADS * HEAD_DIM
N_BUCKETS = 32
MAX_DISTANCE = 128
POOL_WINDOWS = (2, 4, 8, 16)
POOL_GROUP_DIM = 128
POOL_DIM = 512
POOL_TAIL = 16
GMLP_DIM = 512
GMLP_GROUPS = 4
GATE_DIM = 3 * D_MODEL
PEER_HEADS = 8
N_KEYS = 128
PEER_TOPK = 16
PEER_QDIM = 256
PEER_HALF = 128
PEER_SEL = PEER_HEADS * PEER_TOPK
EPS = 1e-6
NEG_INF = -1e30

SUBLANES = 8
LANES = 128
N_EXPERTS = N_KEYS * N_KEYS
HALF_ROWS = SUBLANES // 2
VMEM_LIMIT = 56 * 1024 * 1024


def _rmsnorm(x, gain):
    return x * lax.rsqrt(jnp.mean(x * x, axis=-1, keepdims=True) + EPS) * gain


def _gelu(x):
    c = math.sqrt(2.0 / math.pi)
    return x * (0.5 * (1.0 + jnp.tanh(c * (x + 0.044715 * (x * x * x)))))


def _dot(a, b):
    return jnp.dot(a, b, preferred_element_type=jnp.float32)


def _dot_nt(a, b):
    return lax.dot_general(a, b, (((1,), (1,)), ((), ())), preferred_element_type=jnp.float32)


IN_TM = 256


def _in_proj_kernel(x_ref, gain_ref, w_ref, q_ref, kv_ref, pz_ref, guv_ref, g_ref):
    h = _rmsnorm(x_ref[...], gain_ref[...]).astype(jnp.bfloat16)
    o = 0
    for ref in (q_ref, kv_ref, pz_ref, guv_ref, g_ref):
        n = ref.shape[1]
        ref[...] = _dot(h, w_ref[:, o:o + n]).astype(ref.dtype)
        o += n


def _in_proj(x, gain, w_in):
    T = x.shape[0]
    widths = (Q_DIM, 2 * KV_DIM, POOL_DIM, 2 * GMLP_DIM, GATE_DIM)
    dtypes = (jnp.bfloat16, jnp.bfloat16, jnp.float32, jnp.float32, jnp.float32)
    return pl.pallas_call(
        _in_proj_kernel,
        grid=(T // IN_TM,),
        in_specs=[
            pl.BlockSpec((IN_TM, D_MODEL), lambda i: (i, 0)),
            pl.BlockSpec((1, D_MODEL), lambda i: (0, 0)),
            pl.BlockSpec(w_in.shape, lambda i: (0, 0)),
        ],
        out_specs=[pl.BlockSpec((IN_TM, n), lambda i: (i, 0)) for n in widths],
        out_shape=[jax.ShapeDtypeStruct((T, n), d) for n, d in zip(widths, dtypes)],
        compiler_params=pltpu.CompilerParams(
            dimension_semantics=("parallel",), vmem_limit_bytes=VMEM_LIMIT),
        name="in_proj",
    )(x, gain, w_in)


def _mixer_kernel(sinks_ref, x_ref, q_ref, kvc_ref, kvp_ref, pzc_ref, pzp_ref, guv_ref, g_ref,
                  bias_ref, wpool_ref, pscale_ref, vgain_ref, wsp_ref, bsp_ref,
                  wa_ref, wb_ref, wc_ref, wout_ref, bgate_ref, o_ref):
    i = pl.program_id(1)
    first = i == 0

    kvc = kvc_ref[...]
    kvp = kvp_ref[...]
    kk = jnp.concatenate([kvp[:, :KV_DIM], kvc[:, :KV_DIM]], axis=0)
    vv = jnp.concatenate([kvp[:, KV_DIM:], kvc[:, KV_DIM:]], axis=0)
    qi = lax.broadcasted_iota(jnp.int32, (BLK, 2 * BLK), 0)
    kj = lax.broadcasted_iota(jnp.int32, (BLK, 2 * BLK), 1)
    dist = qi + BLK - kj
    visible = (dist >= 0) & (dist < WINDOW) & jnp.logical_not(first & (kj < BLK))
    q = q_ref[...]
    heads = []
    for h in range(N_HEADS):
        g = h // GQA_GROUP
        qh = q[:, h * HEAD_DIM:(h + 1) * HEAD_DIM]
        kh = kk[:, g * HEAD_DIM:(g + 1) * HEAD_DIM]
        vh = vv[:, g * HEAD_DIM:(g + 1) * HEAD_DIM]
        s = _dot_nt(qh, kh) * (HEAD_DIM ** -0.5) + bias_ref[h]
        s = jnp.where(visible, s, NEG_INF)
        sink = sinks_ref[h]
        m = jnp.maximum(jnp.max(s, axis=-1, keepdims=True), sink)
        p = jnp.exp(s - m)
        denom = jnp.sum(p, axis=-1, keepdims=True) + jnp.exp(sink - m)
        heads.append(_dot(p.astype(jnp.bfloat16), vh) / denom)
    attn = jnp.concatenate(heads, axis=-1).astype(jnp.bfloat16)
    ya = _dot(attn, wa_ref[...])

    zc = pzc_ref[...]
    zp = jnp.where(first, 0.0, pzp_ref[...])
    ext = jnp.concatenate([zp, zc], axis=0)
    pos = i * BLK + lax.broadcasted_iota(jnp.int32, (BLK, 1), 0)
    pooled = []
    for gi, w in enumerate(POOL_WINDOWS):
        lo = gi * POOL_GROUP_DIM
        s = ext[:, lo:lo + POOL_GROUP_DIM]
        width = 1
        while width < w:
            s = s + pltpu.roll(s, width, 0)
            width *= 2
        count = jnp.minimum(pos + 1, w).astype(jnp.float32)
        d = s[POOL_TAIL:] / count - zc[:, lo:lo + POOL_GROUP_DIM]
        pooled.append(_dot(d.astype(jnp.bfloat16), wpool_ref[gi]))
    pool = (jnp.concatenate(pooled, axis=-1) * pscale_ref[...]).astype(jnp.bfloat16)
    yb = _dot(pool, wb_ref[...])

    guv = guv_ref[...]
    u = _gelu(guv[:, :GMLP_DIM])
    v = _rmsnorm(_gelu(guv[:, GMLP_DIM:]), vgain_ref[...]).astype(jnp.bfloat16)
    ti = lax.broadcasted_iota(jnp.int32, (BLK, BLK), 0)
    si = lax.broadcasted_iota(jnp.int32, (BLK, BLK), 1)
    bsp = bsp_ref[...]
    mixed = []
    gd = GMLP_DIM // GMLP_GROUPS
    for gi in range(GMLP_GROUPS):
        w = jnp.where(si <= ti, wsp_ref[gi], 0.0).astype(jnp.bfloat16)
        mixed.append(_dot(w, v[:, gi * gd:(gi + 1) * gd]) + bsp[:, gi:gi + 1])
    gm = (u * jnp.concatenate(mixed, axis=-1)).astype(jnp.bfloat16)
    yc = _dot(gm, wc_ref[...])

    gates = jax.nn.sigmoid(g_ref[...] + bgate_ref[...])
    merged = (gates[:, :D_MODEL] * ya + gates[:, D_MODEL:2 * D_MODEL] * yb
              + gates[:, 2 * D_MODEL:] * yc)
    o_ref[...] = x_ref[...] + _dot(merged.astype(jnp.bfloat16), wout_ref[...])


def _mixer(x, q, kv, pz, guv, g, sinks, bias, wpool, pscale, vgain, wsp, bsp,
           wa, wb, wc, wout, bgate, batch):
    T = x.shape[0]
    nb = T // batch // BLK
    tail_per_blk = BLK // POOL_TAIL

    def cur(b, i):
        return (b * nb + i, 0)

    def prev(b, i):
        return (b * nb + jnp.maximum(i - 1, 0), 0)

    def prev_tail(b, i):
        return (jnp.maximum((b * nb + i) * tail_per_blk - 1, 0), 0)

    def const(shape):
        return pl.BlockSpec(shape, lambda b, i: (0,) * len(shape))

    return pl.pallas_call(
        _mixer_kernel,
        grid=(batch, nb),
        in_specs=[
            pl.BlockSpec(memory_space=pltpu.SMEM),
            pl.BlockSpec((BLK, D_MODEL), cur),
            pl.BlockSpec((BLK, Q_DIM), cur),
            pl.BlockSpec((BLK, 2 * KV_DIM), cur),
            pl.BlockSpec((BLK, 2 * KV_DIM), prev),
            pl.BlockSpec((BLK, POOL_DIM), cur),
            pl.BlockSpec((POOL_TAIL, POOL_DIM), prev_tail),
            pl.BlockSpec((BLK, 2 * GMLP_DIM), cur),
            pl.BlockSpec((BLK, GATE_DIM), cur),
            const(bias.shape), const(wpool.shape), const(pscale.shape), const(vgain.shape),
            const(wsp.shape), const(bsp.shape), const(wa.shape), const(wb.shape),
            const(wc.shape), const(wout.shape), const(bgate.shape),
        ],
        out_specs=pl.BlockSpec((BLK, D_MODEL), cur),
        out_shape=jax.ShapeDtypeStruct((T, D_MODEL), jnp.float32),
        compiler_params=pltpu.CompilerParams(
            dimension_semantics=("parallel", "parallel"), vmem_limit_bytes=VMEM_LIMIT),
        name="mixer",
    )(sinks, x, q, kv, kv, pz, pz, guv, g, bias, wpool, pscale, vgain, wsp, bsp,
      wa, wb, wc, wout, bgate)


def _topk_rows(s, k):
    rows = s.shape[0]
    rid = lax.broadcasted_iota(jnp.int32, s.shape, 0)
    vals, ids = [], []
    for _ in range(k):
        m = jnp.max(s, axis=0, keepdims=True)
        sel = jnp.min(jnp.where(s == m, rid, rows), axis=0, keepdims=True)
        vals.append(m)
        ids.append(sel)
        s = jnp.where(rid == sel, -jnp.inf, s)
    return jnp.concatenate(vals, axis=0), jnp.concatenate(ids, axis=0)


def _pick_rows(table, which, rows):
    out = jnp.zeros(which.shape, table.dtype)
    for r in range(rows):
        out = jnp.where(which == r, table[r:r + 1], out)
    return out


_CAND_WIDTHS = tuple(PEER_TOPK // (a + 1) for a in range(PEER_TOPK))
_CAND_PAD = -sum(_CAND_WIDTHS) % SUBLANES


def _route_kernel(x_ref, gain_ref, wq_ref, k1_ref, k2_ref, h_ref, expert_ref, gate_ref):
    h = _rmsnorm(x_ref[...], gain_ref[...])
    h_ref[...] = h
    qt = _dot_nt(wq_ref[...], h.astype(jnp.bfloat16)).astype(jnp.bfloat16)
    experts = []
    for hd in range(PEER_HEADS):
        base = hd * PEER_QDIM
        s1 = _dot(k1_ref[...], qt[base:base + PEER_HALF])
        s2 = _dot(k2_ref[...], qt[base + PEER_HALF:base + PEER_QDIM])
        v1, i1 = _topk_rows(s1, PEER_TOPK)
        v2, i2 = _topk_rows(s2, PEER_TOPK)
        cand = jnp.concatenate(
            [v1[a:a + 1] + v2[:w] for a, w in enumerate(_CAND_WIDTHS)]
            + [jnp.full((_CAND_PAD, BLK), -jnp.inf, jnp.float32)], axis=0)
        cand_expert = jnp.concatenate(
            [i1[a:a + 1] * N_KEYS + i2[:w] for a, w in enumerate(_CAND_WIDTHS)]
            + [jnp.zeros((_CAND_PAD, BLK), jnp.int32)], axis=0)
        top, ci = _topk_rows(cand, PEER_TOPK)
        experts.append(_pick_rows(cand_expert, ci, sum(_CAND_WIDTHS)))
        ex = jnp.exp(top - top[0:1])
        gate_ref[0, pl.ds(hd * PEER_TOPK, PEER_TOPK), :] = ex / jnp.sum(ex, axis=0, keepdims=True)
    expert_ref[0] = jnp.concatenate(experts, axis=0).T


def _route(x, gain, wq_t, keys1, keys2):
    T = x.shape[0]
    nblk = T // BLK
    sel_spec = pl.BlockSpec((1, PEER_SEL, BLK), lambda i: (i, 0, 0))
    tok_spec = pl.BlockSpec((1, BLK, PEER_SEL), lambda i: (i, 0, 0))
    return pl.pallas_call(
        _route_kernel,
        grid=(nblk,),
        in_specs=[
            pl.BlockSpec((BLK, D_MODEL), lambda i: (i, 0)),
            pl.BlockSpec((1, D_MODEL), lambda i: (0, 0)),
            pl.BlockSpec(wq_t.shape, lambda i: (0, 0)),
            pl.BlockSpec(keys1.shape, lambda i: (0, 0)),
            pl.BlockSpec(keys2.shape, lambda i: (0, 0)),
        ],
        out_specs=[pl.BlockSpec((BLK, D_MODEL), lambda i: (i, 0)), tok_spec, sel_spec],
        out_shape=[
            jax.ShapeDtypeStruct((T, D_MODEL), jnp.float32),
            jax.ShapeDtypeStruct((nblk, BLK, PEER_SEL), jnp.int32),
            jax.ShapeDtypeStruct((nblk, PEER_SEL, BLK), jnp.float32),
        ],
        compiler_params=pltpu.CompilerParams(
            dimension_semantics=("parallel",), vmem_limit_bytes=VMEM_LIMIT),
        name="route",
    )(x, gain, wq_t, keys1, keys2)


def _row_pair(tab_ref, ea, eb):
    words = jnp.concatenate([tab_ref[ea], tab_ref[eb]], axis=0)
    lo = lax.bitcast_convert_type(lax.shift_left(words, 16), jnp.float32)
    hi = lax.bitcast_convert_type(words & jnp.int32(-65536), jnp.float32)
    return lo, hi


def _half_sums(q, sub):
    odd = (sub & 1) != 0
    c0 = jnp.where(odd, q[0] + pltpu.roll(q[0], 1, 0), q[1] + pltpu.roll(q[1], 7, 0))
    c1 = jnp.where(odd, q[2] + pltpu.roll(q[2], 1, 0), q[3] + pltpu.roll(q[3], 7, 0))
    return jnp.where((sub & 2) != 0, c0 + pltpu.roll(c0, 2, 0), c1 + pltpu.roll(c1, 6, 0))


_GROUP_PAIRS = ((3, 7), (2, 6), (1, 5), (0, 4))


N_GROUPS = PEER_SEL // SUBLANES
PAIR_GROUPS = N_GROUPS // 2
U_CHUNK_SHIFT = 0
U_CHUNK_GROUPS = N_GROUPS >> U_CHUNK_SHIFT
V_CHUNK_SHIFT = 1
V_CHUNK_GROUPS = PAIR_GROUPS >> V_CHUNK_SHIFT


def _by_slot(a):
    nblk, _, width = a.shape
    groups = width // SUBLANES
    return (a.reshape(nblk, BLK, groups, SUBLANES).transpose(0, 3, 1, 2)
            .reshape(nblk, SUBLANES, BLK * groups))


def _slot_copies(src_hbm, dst_slots, sems, blk, half):
    n = src_hbm.shape[2]
    return [pltpu.make_async_copy(src_hbm.at[blk, j], dst_slots[j].at[pl.ds(half * n, n)],
                                  sems.at[half, j])
            for j in range(SUBLANES)]


def _fetch_slots(sources):
    i = pl.program_id(0)
    half = i % 2

    @pl.when(i == 0)
    def _():
        for src, dst, sems in sources:
            for copy in _slot_copies(src, dst, sems, 0, 0):
                copy.start()

    for src, dst, sems in sources:
        for copy in _slot_copies(src, dst, sems, i, half):
            copy.wait()

    @pl.when(i + 1 < pl.num_programs(0))
    def _():
        for src, dst, sems in sources:
            for copy in _slot_copies(src, dst, sems, i + 1, 1 - half):
                copy.start()

    return half


def _bf16_bits(x):
    b = lax.bitcast_convert_type(x, jnp.int32)
    return (b + 0x7FFF + ((b >> 16) & 1)) & jnp.int32(-65536)


def _peer_u_kernel(expert_hbm, h_ref, gate_ref, tab_ref, coeff_ref, *scratch):
    expert_s = scratch[:SUBLANES]
    sems, tiles_ref, act_ref = scratch[SUBLANES:]
    base = _fetch_slots([(expert_hbm, expert_s, sems)]) * (BLK * N_GROUPS)
    sub = lax.broadcasted_iota(jnp.int32, (SUBLANES, LANES), 0)
    chunks = N_GROUPS // U_CHUNK_GROUPS
    rows = U_CHUNK_GROUPS * SUBLANES
    lane = lax.broadcasted_iota(jnp.int32, (rows, BLK), 1)
    tiles_ref[...] = jnp.zeros(tiles_ref.shape, jnp.float32)
    act_ref[...] = jnp.zeros(act_ref.shape, jnp.float32)

    def reduce_trip(m):
        src = pl.ds(pl.multiple_of((m & 1) * rows, rows), rows)
        dst = pl.ds(pl.multiple_of((m & (chunks - 1)) * rows, rows), rows)
        s = jnp.sum(tiles_ref[src, :], axis=1, keepdims=True)
        act_ref[dst, :] = jnp.where(lane == (m >> U_CHUNK_SHIFT), s, act_ref[dst, :])

    def chunk(n, carry):
        reduce_trip(n - 1)
        ht = h_ref[n >> U_CHUNK_SHIFT]
        h_lo = jnp.concatenate([ht[:HALF_ROWS], ht[:HALF_ROWS]], axis=0)
        h_hi = jnp.concatenate([ht[HALF_ROWS:], ht[HALF_ROWS:]], axis=0)
        for grp in range(U_CHUNK_GROUPS):
            r = base + n * U_CHUNK_GROUPS + grp
            q = []
            for a, b in _GROUP_PAIRS:
                lo, hi = _row_pair(tab_ref, expert_s[a][r], expert_s[b][r])
                q.append(lo * h_lo + hi * h_hi)
            dst = (n & 1) * rows + grp * SUBLANES
            tiles_ref[pl.ds(pl.multiple_of(dst, SUBLANES), SUBLANES), :] = _half_sums(q, sub)
        return carry

    lax.fori_loop(0, BLK * chunks, chunk, 0)
    reduce_trip(BLK * chunks - 1)
    coeff = gate_ref[0] * _gelu(act_ref[...])
    words = _bf16_bits(coeff[PEER_SEL // 2:]) | lax.shift_right_logical(
        _bf16_bits(coeff[:PEER_SEL // 2]), 16)
    coeff_ref[0] = words.T


def _peer_v_kernel(expert_hbm, coeff_hbm, x_ref, tab_ref, o_ref, *scratch):
    expert_s, coeff_s = scratch[:SUBLANES], scratch[SUBLANES:2 * SUBLANES]
    expert_sems, coeff_sems = scratch[2 * SUBLANES:]
    half = _fetch_slots([(expert_hbm, expert_s, expert_sems), (coeff_hbm, coeff_s, coeff_sems)])
    expert_base = half * (BLK * N_GROUPS)
    coeff_base = half * (BLK * PAIR_GROUPS)
    first_half = lax.broadcasted_iota(jnp.int32, (SUBLANES, LANES), 0) < HALF_ROWS
    o_ref[...] = x_ref[...]
    chunks = PAIR_GROUPS // V_CHUNK_GROUPS

    def chunk(n, carry):
        t = n >> V_CHUNK_SHIFT
        g0 = (n & (chunks - 1)) * V_CHUNK_GROUPS
        acc_lo = [jnp.zeros((SUBLANES, LANES), jnp.float32) for _ in range(2)]
        acc_hi = [jnp.zeros((SUBLANES, LANES), jnp.float32) for _ in range(2)]
        for grp in range(V_CHUNK_GROUPS):
            ra = expert_base + t * N_GROUPS + g0 + grp
            rb = ra + PAIR_GROUPS
            rc = coeff_base + t * PAIR_GROUPS + g0 + grp
            for j in range(SUBLANES):
                lo, hi = _row_pair(tab_ref, expert_s[j][ra], expert_s[j][rb])
                word = jnp.full((SUBLANES, LANES), coeff_s[j][rc], jnp.int32)
                c = lax.bitcast_convert_type(
                    jnp.where(first_half, lax.shift_left(word, 16), word & jnp.int32(-65536)),
                    jnp.float32)
                acc_lo[j % 2] = acc_lo[j % 2] + c * lo
                acc_hi[j % 2] = acc_hi[j % 2] + c * hi
        y_lo = acc_lo[0] + acc_lo[1]
        y_hi = acc_hi[0] + acc_hi[1]
        y = jnp.concatenate([y_lo[:HALF_ROWS] + y_lo[HALF_ROWS:],
                             y_hi[:HALF_ROWS] + y_hi[HALF_ROWS:]], axis=0)
        o_ref[t] = o_ref[t] + y
        return carry

    lax.fori_loop(0, BLK * chunks, chunk, 0)


def _table_spec(tab):
    return pl.BlockSpec(tab.shape, lambda i: (0, 0, 0), pipeline_mode=pl.Buffered(1))


def _slot_scratch(groups):
    return [pltpu.SMEM((2 * BLK * groups,), jnp.int32) for _ in range(SUBLANES)]


def _slot_sems():
    return pltpu.SemaphoreType.DMA((2, SUBLANES))


def _peer_u(expert, h3, gate, tab):
    nblk = expert.shape[0]
    return pl.pallas_call(
        _peer_u_kernel,
        grid=(nblk,),
        in_specs=[
            pl.BlockSpec(memory_space=pl.ANY),
            pl.BlockSpec((BLK, SUBLANES, LANES), lambda i: (i, 0, 0)),
            pl.BlockSpec((1, PEER_SEL, BLK), lambda i: (i, 0, 0)),
            _table_spec(tab),
        ],
        out_specs=pl.BlockSpec((1, BLK, PEER_SEL // 2), lambda i: (i, 0, 0)),
        out_shape=jax.ShapeDtypeStruct((nblk, BLK, PEER_SEL // 2), jnp.int32),
        scratch_shapes=_slot_scratch(N_GROUPS) + [
            _slot_sems(),
            pltpu.VMEM((2 * U_CHUNK_GROUPS * SUBLANES, LANES), jnp.float32),
            pltpu.VMEM((PEER_SEL, BLK), jnp.float32),
        ],
        compiler_params=pltpu.CompilerParams(
            dimension_semantics=("arbitrary",), vmem_limit_bytes=VMEM_LIMIT),
        name="peer_u",
    )(expert, h3, gate, tab)


def _peer_v(expert, coeff, x3, tab):
    nblk = expert.shape[0]
    tok_spec = pl.BlockSpec((BLK, SUBLANES, LANES), lambda i: (i, 0, 0))
    return pl.pallas_call(
        _peer_v_kernel,
        grid=(nblk,),
        in_specs=[
            pl.BlockSpec(memory_space=pl.ANY),
            pl.BlockSpec(memory_space=pl.ANY),
            tok_spec,
            _table_spec(tab),
        ],
        out_specs=tok_spec,
        out_shape=jax.ShapeDtypeStruct(x3.shape, jnp.float32),
        scratch_shapes=(_slot_scratch(N_GROUPS) + _slot_scratch(PAIR_GROUPS)
                        + [_slot_sems(), _slot_sems()]),
        compiler_params=pltpu.CompilerParams(
            dimension_semantics=("arbitrary",), vmem_limit_bytes=VMEM_LIMIT),
        name="peer_v",
    )(expert, coeff, x3, tab)


NORM_TM = 512


def _final_norm_kernel(x_ref, gain_ref, o_ref):
    o_ref[...] = _rmsnorm(x_ref[...], gain_ref[...])


def _final_norm(x, gain):
    T = x.shape[0]
    return pl.pallas_call(
        _final_norm_kernel,
        grid=(T // NORM_TM,),
        in_specs=[pl.BlockSpec((NORM_TM, D_MODEL), lambda i: (i, 0)),
                  pl.BlockSpec((1, D_MODEL), lambda i: (0, 0))],
        out_specs=pl.BlockSpec((NORM_TM, D_MODEL), lambda i: (i, 0)),
        out_shape=jax.ShapeDtypeStruct((T, D_MODEL), jnp.float32),
        compiler_params=pltpu.CompilerParams(dimension_semantics=("parallel",)),
        name="final_norm",
    )(x, gain)


def _t5_causal_bucket(dist):
    n = jnp.maximum(dist, 0)
    max_exact = N_BUCKETS // 2
    nf = jnp.maximum(n, 1).astype(jnp.float32)
    large = max_exact + (jnp.log(nf / max_exact) / math.log(MAX_DISTANCE / max_exact)
                         * (N_BUCKETS - max_exact)).astype(jnp.int32)
    large = jnp.minimum(large, N_BUCKETS - 1)
    return jnp.where(n < max_exact, n, large)


def _attention_bias(rel_bias):
    qi = jnp.arange(BLK)[:, None]
    kj = jnp.arange(2 * BLK)[None, :]
    bias = rel_bias[_t5_causal_bucket(qi + BLK - kj)].astype(jnp.float32)
    return bias.transpose(2, 0, 1)


def _pack_table(tab):
    bits = lax.bitcast_convert_type(tab.astype(jnp.bfloat16), jnp.uint16).astype(jnp.uint32)
    words = bits[:, :D_MODEL // 2] | (bits[:, D_MODEL // 2:] << 16)
    return lax.bitcast_convert_type(words, jnp.int32).reshape(N_EXPERTS, HALF_ROWS, LANES)


def kernel(x, rel_bias, final_gain, attn_norm_gain, w_in, b_gate, attn_sinks, w_pool, pool_scale, gmlp_v_gain, w_spatial, b_spatial, w_branch_a, w_branch_b, w_branch_c, w_out, ffn_norm_gain, peer_w_query, peer_sub_keys1, peer_sub_keys2, peer_expert_u, peer_expert_v):
    B, S, D = x.shape
    T = B * S
    depth = w_in.shape[0]
    bf16 = jnp.bfloat16
    bias = _attention_bias(rel_bias)
    xt = x.reshape(T, D)
    for l in range(depth):
        q, kv, pz, guv, g = _in_proj(xt, attn_norm_gain[l][None], w_in[l].astype(bf16))
        xt = _mixer(xt, q, kv, pz, guv, g, attn_sinks[l], bias,
                    w_pool[l].astype(bf16), pool_scale[l][None], gmlp_v_gain[l][None],
                    w_spatial[l], b_spatial[l].T,
                    w_branch_a[l].astype(bf16), w_branch_b[l].astype(bf16),
                    w_branch_c[l].astype(bf16), w_out[l].astype(bf16), b_gate[l][None], B)
        h, expert, gate = _route(xt, ffn_norm_gain[l][None], peer_w_query[l].T.astype(bf16),
                                 peer_sub_keys1[l].astype(bf16), peer_sub_keys2[l].astype(bf16))
        expert = _by_slot(expert)
        coeff = _peer_u(expert, h.reshape(T, SUBLANES, LANES), gate, _pack_table(peer_expert_u[l]))
        xt = _peer_v(expert, _by_slot(coeff), xt.reshape(T, SUBLANES, LANES),
                     _pack_table(peer_expert_v[l])).reshape(T, D)
    return _final_norm(xt, final_gain[None]).reshape(B, S, D)
```

```python
import functools
import math

import jax
import jax.numpy as jnp
from jax import lax
from jax.experimental import pallas as pl
from jax.experimental.pallas import tpu as pltpu

D_MODEL = 1024
N_HEADS = 8
N_KV_HEADS = 2
HEAD_DIM = 64
GQA_GROUP = N_HEADS // N_KV_HEADS
WINDOW = 128
BLK = 128
Q_DIM = N_HEADS * HEAD_DIM
KV_DIM = N_KV_HEADS * HEAD_DIM
N_BUCKETS = 32
MAX_DISTANCE = 128
POOL_WINDOWS = (2, 4, 8, 16)
POOL_GROUP_DIM = 128
POOL_DIM = 512
POOL_TAIL = 16
GMLP_DIM = 512
GMLP_GROUPS = 4
GATE_DIM = 3 * D_MODEL
PEER_HEADS = 8
N_KEYS = 128
PEER_TOPK = 16
PEER_QDIM = 256
PEER_HALF = 128
PEER_SEL = PEER_HEADS * PEER_TOPK
EPS = 1e-6
NEG_INF = -1e30

SUBLANES = 8
LANES = 128
N_EXPERTS = N_KEYS * N_KEYS
HALF_ROWS = SUBLANES // 2
VMEM_LIMIT = 56 * 1024 * 1024


def _rmsnorm(x, gain):
    return x * lax.rsqrt(jnp.mean(x * x, axis=-1, keepdims=True) + EPS) * gain


def _gelu(x):
    c = math.sqrt(2.0 / math.pi)
    return x * (0.5 * (1.0 + jnp.tanh(c * (x + 0.044715 * (x * x * x)))))


def _dot(a, b):
    return jnp.dot(a, b, preferred_element_type=jnp.float32)


def _dot_nt(a, b):
    return lax.dot_general(a, b, (((1,), (1,)), ((), ())), preferred_element_type=jnp.float32)


IN_TM = 256


def _in_proj_kernel(x_ref, gain_ref, w_ref, q_ref, kv_ref, pz_ref, guv_ref, g_ref):
    h = _rmsnorm(x_ref[...], gain_ref[...]).astype(jnp.bfloat16)
    o = 0
    for ref in (q_ref, kv_ref, pz_ref, guv_ref, g_ref):
        n = ref.shape[1]
        ref[...] = _dot(h, w_ref[:, o:o + n]).astype(ref.dtype)
        o += n


def _in_proj(x, gain, w_in):
    T = x.shape[0]
    widths = (Q_DIM, 2 * KV_DIM, POOL_DIM, 2 * GMLP_DIM, GATE_DIM)
    dtypes = (jnp.bfloat16, jnp.bfloat16, jnp.float32, jnp.float32, jnp.float32)
    return pl.pallas_call(
        _in_proj_kernel,
        grid=(T // IN_TM,),
        in_specs=[
            pl.BlockSpec((IN_TM, D_MODEL), lambda i: (i, 0)),
            pl.BlockSpec((1, D_MODEL), lambda i: (0, 0)),
            pl.BlockSpec(w_in.shape, lambda i: (0, 0)),
        ],
        out_specs=[pl.BlockSpec((IN_TM, n), lambda i: (i, 0)) for n in widths],
        out_shape=[jax.ShapeDtypeStruct((T, n), d) for n, d in zip(widths, dtypes)],
        compiler_params=pltpu.CompilerParams(
            dimension_semantics=("parallel",), vmem_limit_bytes=VMEM_LIMIT),
        name="in_proj",
    )(x, gain, w_in)


def _mixer_kernel(sinks_ref, x_ref, q_ref, kvc_ref, kvp_ref, pzc_ref, pzp_ref, guv_ref, g_ref,
                  bias_ref, wpool_ref, pscale_ref, vgain_ref, wsp_ref, bsp_ref,
                  wa_ref, wb_ref, wc_ref, wout_ref, bgate_ref, o_ref):
    i = pl.program_id(1)
    first = i == 0

    kvc = kvc_ref[...]
    kvp = kvp_ref[...]
    kk = jnp.concatenate([kvp[:, :KV_DIM], kvc[:, :KV_DIM]], axis=0)
    vv = jnp.concatenate([kvp[:, KV_DIM:], kvc[:, KV_DIM:]], axis=0)
    qi = lax.broadcasted_iota(jnp.int32, (BLK, 2 * BLK), 0)
    kj = lax.broadcasted_iota(jnp.int32, (BLK, 2 * BLK), 1)
    dist = qi + BLK - kj
    visible = (dist >= 0) & (dist < WINDOW) & jnp.logical_not(first & (kj < BLK))
    q = q_ref[...]
    heads = []
    for h in range(N_HEADS):
        g = h // GQA_GROUP
        qh = q[:, h * HEAD_DIM:(h + 1) * HEAD_DIM]
        kh = kk[:, g * HEAD_DIM:(g + 1) * HEAD_DIM]
        vh = vv[:, g * HEAD_DIM:(g + 1) * HEAD_DIM]
        s = _dot_nt(qh, kh) * (HEAD_DIM ** -0.5) + bias_ref[h]
        s = jnp.where(visible, s, NEG_INF)
        sink = sinks_ref[h]
        m = jnp.maximum(jnp.max(s, axis=-1, keepdims=True), sink)
        p = jnp.exp(s - m)
        denom = jnp.sum(p, axis=-1, keepdims=True) + jnp.exp(sink - m)
        heads.append(_dot(p.astype(jnp.bfloat16), vh) / denom)
    attn = jnp.concatenate(heads, axis=-1).astype(jnp.bfloat16)
    ya = _dot(attn, wa_ref[...])

    zc = pzc_ref[...]
    zp = jnp.where(first, 0.0, pzp_ref[...])
    ext = jnp.concatenate([zp, zc], axis=0)
    pos = i * BLK + lax.broadcasted_iota(jnp.int32, (BLK, 1), 0)
    pooled = []
    for gi, w in enumerate(POOL_WINDOWS):
        lo = gi * POOL_GROUP_DIM
        s = ext[:, lo:lo + POOL_GROUP_DIM]
        width = 1
        while width < w:
            s = s + pltpu.roll(s, width, 0)
            width *= 2
        count = jnp.minimum(pos + 1, w).astype(jnp.float32)
        d = s[POOL_TAIL:] / count - zc[:, lo:lo + POOL_GROUP_DIM]
        pooled.append(_dot(d.astype(jnp.bfloat16), wpool_ref[gi]))
    pool = (jnp.concatenate(pooled, axis=-1) * pscale_ref[...]).astype(jnp.bfloat16)
    yb = _dot(pool, wb_ref[...])

    guv = guv_ref[...]
    u = _gelu(guv[:, :GMLP_DIM])
    v = _rmsnorm(_gelu(guv[:, GMLP_DIM:]), vgain_ref[...]).astype(jnp.bfloat16)
    ti = lax.broadcasted_iota(jnp.int32, (BLK, BLK), 0)
    si = lax.broadcasted_iota(jnp.int32, (BLK, BLK), 1)
    bsp = bsp_ref[...]
    mixed = []
    gd = GMLP_DIM // GMLP_GROUPS
    for gi in range(GMLP_GROUPS):
        w = jnp.where(si <= ti, wsp_ref[gi], 0.0).astype(jnp.bfloat16)
        mixed.append(_dot(w, v[:, gi * gd:(gi + 1) * gd]) + bsp[:, gi:gi + 1])
    gm = (u * jnp.concatenate(mixed, axis=-1)).astype(jnp.bfloat16)
    yc = _dot(gm, wc_ref[...])

    gates = jax.nn.sigmoid(g_ref[...] + bgate_ref[...])
    merged = (gates[:, :D_MODEL] * ya + gates[:, D_MODEL:2 * D_MODEL] * yb
              + gates[:, 2 * D_MODEL:] * yc)
    o_ref[...] = x_ref[...] + _dot(merged.astype(jnp.bfloat16), wout_ref[...])


def _mixer(x, q, kv, pz, guv, g, sinks, bias, wpool, pscale, vgain, wsp, bsp,
           wa, wb, wc, wout, bgate, batch):
    T = x.shape[0]
    nb = T // batch // BLK
    tail_per_blk = BLK // POOL_TAIL

    def cur(b, i):
        return (b * nb + i, 0)

    def prev(b, i):
        return (b * nb + jnp.maximum(i - 1, 0), 0)

    def prev_tail(b, i):
        return (jnp.maximum((b * nb + i) * tail_per_blk - 1, 0), 0)

    def const(shape):
        return pl.BlockSpec(shape, lambda b, i: (0,) * len(shape))

    return pl.pallas_call(
        _mixer_kernel,
        grid=(batch, nb),
        in_specs=[
            pl.BlockSpec(memory_space=pltpu.SMEM),
            pl.BlockSpec((BLK, D_MODEL), cur),
            pl.BlockSpec((BLK, Q_DIM), cur),
            pl.BlockSpec((BLK, 2 * KV_DIM), cur),
            pl.BlockSpec((BLK, 2 * KV_DIM), prev),
            pl.BlockSpec((BLK, POOL_DIM), cur),
            pl.BlockSpec((POOL_TAIL, POOL_DIM), prev_tail),
            pl.BlockSpec((BLK, 2 * GMLP_DIM), cur),
            pl.BlockSpec((BLK, GATE_DIM), cur),
            const(bias.shape), const(wpool.shape), const(pscale.shape), const(vgain.shape),
            const(wsp.shape), const(bsp.shape), const(wa.shape), const(wb.shape),
            const(wc.shape), const(wout.shape), const(bgate.shape),
        ],
        out_specs=pl.BlockSpec((BLK, D_MODEL), cur),
        out_shape=jax.ShapeDtypeStruct((T, D_MODEL), jnp.float32),
        compiler_params=pltpu.CompilerParams(
            dimension_semantics=("parallel", "parallel"), vmem_limit_bytes=VMEM_LIMIT),
        name="mixer",
    )(sinks, x, q, kv, kv, pz, pz, guv, g, bias, wpool, pscale, vgain, wsp, bsp,
      wa, wb, wc, wout, bgate)


def _topk_rows(s, k):
    rows = s.shape[0]
    rid = lax.broadcasted_iota(jnp.int32, s.shape, 0)
    vals, ids = [], []
    for _ in range(k):
        m = jnp.max(s, axis=0, keepdims=True)
        sel = jnp.min(jnp.where(s == m, rid, rows), axis=0, keepdims=True)
        vals.append(m)
        ids.append(sel)
        s = jnp.where(rid == sel, -jnp.inf, s)
    return jnp.concatenate(vals, axis=0), jnp.concatenate(ids, axis=0)


def _pick_rows(table, which, rows):
    out = jnp.zeros(which.shape, table.dtype)
    for r in range(rows):
        out = jnp.where(which == r, table[r:r + 1], out)
    return out


_CAND_WIDTHS = tuple(PEER_TOPK // (a + 1) for a in range(PEER_TOPK))
_CAND_PAD = -sum(_CAND_WIDTHS) % SUBLANES


def _route_kernel(x_ref, gain_ref, wq_ref, k1_ref, k2_ref, h_ref, expert_ref, gate_ref):
    h = _rmsnorm(x_ref[...], gain_ref[...])
    h_ref[...] = h
    qt = _dot_nt(wq_ref[...], h.astype(jnp.bfloat16)).astype(jnp.bfloat16)
    experts = []
    for hd in range(PEER_HEADS):
        base = hd * PEER_QDIM
        s1 = _dot(k1_ref[...], qt[base:base + PEER_HALF])
        s2 = _dot(k2_ref[...], qt[base + PEER_HALF:base + PEER_QDIM])
        v1, i1 = _topk_rows(s1, PEER_TOPK)
        v2, i2 = _topk_rows(s2, PEER_TOPK)
        cand = jnp.concatenate(
            [v1[a:a + 1] + v2[:w] for a, w in enumerate(_CAND_WIDTHS)]
            + [jnp.full((_CAND_PAD, BLK), -jnp.inf, jnp.float32)], axis=0)
        cand_expert = jnp.concatenate(
            [i1[a:a + 1] * N_KEYS + i2[:w] for a, w in enumerate(_CAND_WIDTHS)]
            + [jnp.zeros((_CAND_PAD, BLK), jnp.int32)], axis=0)
        top, ci = _topk_rows(cand, PEER_TOPK)
        rank = lax.broadcasted_iota(jnp.int32, (PEER_TOPK, BLK), 0)
        experts.append(_pick_rows(cand_expert, ci, sum(_CAND_WIDTHS)) * HALF_ROWS
                       + (TABLE_PAD - (rank & HALF_ROWS)))
        ex = jnp.exp(top - top[0:1])
        gate_ref[0, pl.ds(hd * PEER_TOPK, PEER_TOPK), :] = ex / jnp.sum(ex, axis=0, keepdims=True)
    expert_ref[0] = jnp.concatenate(experts, axis=0).T


def _route(x, gain, wq_t, keys1, keys2):
    T = x.shape[0]
    nblk = T // BLK
    sel_spec = pl.BlockSpec((1, PEER_SEL, BLK), lambda i: (i, 0, 0))
    tok_spec = pl.BlockSpec((1, BLK, PEER_SEL), lambda i: (i, 0, 0))
    return pl.pallas_call(
        _route_kernel,
        grid=(nblk,),
        in_specs=[
            pl.BlockSpec((BLK, D_MODEL), lambda i: (i, 0)),
            pl.BlockSpec((1, D_MODEL), lambda i: (0, 0)),
            pl.BlockSpec(wq_t.shape, lambda i: (0, 0)),
            pl.BlockSpec(keys1.shape, lambda i: (0, 0)),
            pl.BlockSpec(keys2.shape, lambda i: (0, 0)),
        ],
        out_specs=[pl.BlockSpec((BLK, D_MODEL), lambda i: (i, 0)), tok_spec, sel_spec],
        out_shape=[
            jax.ShapeDtypeStruct((T, D_MODEL), jnp.float32),
            jax.ShapeDtypeStruct((nblk, BLK, PEER_SEL), jnp.int32),
            jax.ShapeDtypeStruct((nblk, PEER_SEL, BLK), jnp.float32),
        ],
        compiler_params=pltpu.CompilerParams(
            dimension_semantics=("parallel",), vmem_limit_bytes=VMEM_LIMIT),
        name="route",
    )(x, gain, wq_t, keys1, keys2)


def _row_pair(tab_ref, wa, wb, first_half):
    words = jnp.where(first_half,
                      tab_ref[pl.ds(pl.multiple_of(wa, HALF_ROWS), SUBLANES), :],
                      tab_ref[pl.ds(pl.multiple_of(wb, HALF_ROWS), SUBLANES), :])
    lo = lax.bitcast_convert_type(lax.shift_left(words, 16), jnp.float32)
    hi = lax.bitcast_convert_type(words & jnp.int32(-65536), jnp.float32)
    return lo, hi


def _half_sums(q, sub):
    odd = (sub & 1) != 0
    c0 = jnp.where(odd, q[0] + pltpu.roll(q[0], 1, 0), q[1] + pltpu.roll(q[1], 7, 0))
    c1 = jnp.where(odd, q[2] + pltpu.roll(q[2], 1, 0), q[3] + pltpu.roll(q[3], 7, 0))
    return jnp.where((sub & 2) != 0, c0 + pltpu.roll(c0, 2, 0), c1 + pltpu.roll(c1, 6, 0))


TABLE_PAD = HALF_ROWS


N_GROUPS = PEER_SEL // SUBLANES
V_CHUNK_SHIFT = 1
V_CHUNK_GROUPS = N_GROUPS >> V_CHUNK_SHIFT


def _by_slot(a):
    nblk, _, width = a.shape
    groups = width // SUBLANES
    return (a.reshape(nblk, BLK, groups, SUBLANES).transpose(0, 3, 1, 2)
            .reshape(nblk, SUBLANES, BLK * groups))


def _slot_copies(src_hbm, dst_slots, sems, blk, half):
    n = src_hbm.shape[2]
    return [pltpu.make_async_copy(src_hbm.at[blk, j], dst_slots[j].at[pl.ds(half * n, n)],
                                  sems.at[half, j])
            for j in range(SUBLANES)]


def _fetch_slots(sources):
    i = pl.program_id(0)
    half = i % 2

    @pl.when(i == 0)
    def _():
        for src, dst, sems in sources:
            for copy in _slot_copies(src, dst, sems, 0, 0):
                copy.start()

    for src, dst, sems in sources:
        for copy in _slot_copies(src, dst, sems, i, half):
            copy.wait()

    @pl.when(i + 1 < pl.num_programs(0))
    def _():
        for src, dst, sems in sources:
            for copy in _slot_copies(src, dst, sems, i + 1, 1 - half):
                copy.start()

    return half


def _bf16_bits(x):
    b = lax.bitcast_convert_type(x, jnp.int32)
    return (b + 0x7FFF + ((b >> 16) & 1)) & jnp.int32(-65536)


def _peer_u_kernel(expert_hbm, h_ref, gate_ref, tab_ref, coeff_ref, *scratch):
    expert_s = scratch[:SUBLANES]
    sems, tiles_ref, act_ref = scratch[SUBLANES:]
    base = _fetch_slots([(expert_hbm, expert_s, sems)]) * (BLK * N_GROUPS)
    sub = lax.broadcasted_iota(jnp.int32, (SUBLANES, LANES), 0)
    first_half = sub < HALF_ROWS
    lane = lax.broadcasted_iota(jnp.int32, (PEER_SEL, BLK), 1)
    tiles_ref[...] = jnp.zeros(tiles_ref.shape, jnp.float32)
    act_ref[...] = jnp.zeros(act_ref.shape, jnp.float32)

    def reduce_token(t):
        src = pl.ds(pl.multiple_of((t & 1) * PEER_SEL, PEER_SEL), PEER_SEL)
        s = jnp.sum(tiles_ref[src, :], axis=1, keepdims=True)
        act_ref[...] = jnp.where(lane == t, s, act_ref[...])

    def token(t, carry):
        reduce_token(t - 1)
        ht = h_ref[t]
        h_lo = jnp.concatenate([ht[:HALF_ROWS], ht[:HALF_ROWS]], axis=0)
        h_hi = jnp.concatenate([ht[HALF_ROWS:], ht[HALF_ROWS:]], axis=0)
        for grp in range(N_GROUPS):
            r = base + t * N_GROUPS + grp
            q = []
            for i in reversed(range(HALF_ROWS)):
                lo, hi = _row_pair(tab_ref, expert_s[i][r], expert_s[i + HALF_ROWS][r], first_half)
                q.append(lo * h_lo + hi * h_hi)
            dst = (t & 1) * PEER_SEL + grp * SUBLANES
            tiles_ref[pl.ds(pl.multiple_of(dst, SUBLANES), SUBLANES), :] = _half_sums(q, sub)
        return carry

    lax.fori_loop(0, BLK, token, 0)
    reduce_token(BLK - 1)
    coeff = gate_ref[0] * _gelu(act_ref[...])
    partner = pltpu.roll(coeff, PEER_SEL - HALF_ROWS, 0)
    coeff_ref[0] = (_bf16_bits(partner) | lax.shift_right_logical(_bf16_bits(coeff), 16)).T


def _peer_v_kernel(expert_hbm, coeff_hbm, x_ref, tab_ref, o_ref, *scratch):
    expert_s, coeff_s = scratch[:SUBLANES], scratch[SUBLANES:2 * SUBLANES]
    expert_sems, coeff_sems = scratch[2 * SUBLANES:]
    base = _fetch_slots([(expert_hbm, expert_s, expert_sems),
                         (coeff_hbm, coeff_s, coeff_sems)]) * (BLK * N_GROUPS)
    first_half = lax.broadcasted_iota(jnp.int32, (SUBLANES, LANES), 0) < HALF_ROWS
    coeff_shift = jnp.where(first_half, 16, 0)
    o_ref[...] = x_ref[...]

    def chunk(n, carry):
        acc_lo = [jnp.zeros((SUBLANES, LANES), jnp.float32) for _ in range(2)]
        acc_hi = [jnp.zeros((SUBLANES, LANES), jnp.float32) for _ in range(2)]
        for grp in range(V_CHUNK_GROUPS):
            r = base + n * V_CHUNK_GROUPS + grp
            for i in range(HALF_ROWS):
                lo, hi = _row_pair(tab_ref, expert_s[i][r], expert_s[i + HALF_ROWS][r], first_half)
                word = jnp.full((SUBLANES, LANES), coeff_s[i][r], jnp.int32)
                c = lax.bitcast_convert_type(
                    lax.shift_left(word, coeff_shift) & jnp.int32(-65536), jnp.float32)
                acc_lo[i % 2] = acc_lo[i % 2] + c * lo
                acc_hi[i % 2] = acc_hi[i % 2] + c * hi
        y_lo = acc_lo[0] + acc_lo[1]
        y_hi = acc_hi[0] + acc_hi[1]
        y = jnp.concatenate([y_lo[:HALF_ROWS] + y_lo[HALF_ROWS:],
                             y_hi[:HALF_ROWS] + y_hi[HALF_ROWS:]], axis=0)
        t = n >> V_CHUNK_SHIFT
        o_ref[t] = o_ref[t] + y
        return carry

    lax.fori_loop(0, BLK << V_CHUNK_SHIFT, chunk, 0)


def _table_spec(tab):
    return pl.BlockSpec(tab.shape, lambda i: (0, 0), pipeline_mode=pl.Buffered(1))


def _slot_scratch(groups):
    return [pltpu.SMEM((2 * BLK * groups,), jnp.int32) for _ in range(SUBLANES)]


def _slot_sems():
    return pltpu.SemaphoreType.DMA((2, SUBLANES))


def _peer_u(expert, h3, gate, tab):
    nblk = expert.shape[0]
    return pl.pallas_call(
        _peer_u_kernel,
        grid=(nblk,),
        in_specs=[
            pl.BlockSpec(memory_space=pl.ANY),
            pl.BlockSpec((BLK, SUBLANES, LANES), lambda i: (i, 0, 0)),
            pl.BlockSpec((1, PEER_SEL, BLK), lambda i: (i, 0, 0)),
            _table_spec(tab),
        ],
        out_specs=pl.BlockSpec((1, BLK, PEER_SEL), lambda i: (i, 0, 0)),
        out_shape=jax.ShapeDtypeStruct((nblk, BLK, PEER_SEL), jnp.int32),
        scratch_shapes=_slot_scratch(N_GROUPS) + [
            _slot_sems(),
            pltpu.VMEM((2 * PEER_SEL, LANES), jnp.float32),
            pltpu.VMEM((PEER_SEL, BLK), jnp.float32),
        ],
        compiler_params=pltpu.CompilerParams(
            dimension_semantics=("arbitrary",), vmem_limit_bytes=VMEM_LIMIT),
        name="peer_u",
    )(expert, h3, gate, tab)


def _peer_v(expert, coeff, x3, tab):
    nblk = expert.shape[0]
    tok_spec = pl.BlockSpec((BLK, SUBLANES, LANES), lambda i: (i, 0, 0))
    return pl.pallas_call(
        _peer_v_kernel,
        grid=(nblk,),
        in_specs=[
            pl.BlockSpec(memory_space=pl.ANY),
            pl.BlockSpec(memory_space=pl.ANY),
            tok_spec,
            _table_spec(tab),
        ],
        out_specs=tok_spec,
        out_shape=jax.ShapeDtypeStruct(x3.shape, jnp.float32),
        scratch_shapes=(_slot_scratch(N_GROUPS) + _slot_scratch(N_GROUPS)
                        + [_slot_sems(), _slot_sems()]),
        compiler_params=pltpu.CompilerParams(
            dimension_semantics=("arbitrary",), vmem_limit_bytes=VMEM_LIMIT),
        name="peer_v",
    )(expert, coeff, x3, tab)


NORM_TM = 512


def _final_norm_kernel(x_ref, gain_ref, o_ref):
    o_ref[...] = _rmsnorm(x_ref[...], gain_ref[...])


def _final_norm(x, gain):
    T = x.shape[0]
    return pl.pallas_call(
        _final_norm_kernel,
        grid=(T // NORM_TM,),
        in_specs=[pl.BlockSpec((NORM_TM, D_MODEL), lambda i: (i, 0)),
                  pl.BlockSpec((1, D_MODEL), lambda i: (0, 0))],
        out_specs=pl.BlockSpec((NORM_TM, D_MODEL), lambda i: (i, 0)),
        out_shape=jax.ShapeDtypeStruct((T, D_MODEL), jnp.float32),
        compiler_params=pltpu.CompilerParams(dimension_semantics=("parallel",)),
        name="final_norm",
    )(x, gain)


def _t5_causal_bucket(dist):
    n = jnp.maximum(dist, 0)
    max_exact = N_BUCKETS // 2
    nf = jnp.maximum(n, 1).astype(jnp.float32)
    large = max_exact + (jnp.log(nf / max_exact) / math.log(MAX_DISTANCE / max_exact)
                         * (N_BUCKETS - max_exact)).astype(jnp.int32)
    large = jnp.minimum(large, N_BUCKETS - 1)
    return jnp.where(n < max_exact, n, large)


def _attention_bias(rel_bias):
    qi = jnp.arange(BLK)[:, None]
    kj = jnp.arange(2 * BLK)[None, :]
    bias = rel_bias[_t5_causal_bucket(qi + BLK - kj)].astype(jnp.float32)
    return bias.transpose(2, 0, 1)


def _pack_table(tab):
    bits = lax.bitcast_convert_type(tab.astype(jnp.bfloat16), jnp.uint16).astype(jnp.uint32)
    words = bits[:, :D_MODEL // 2] | (bits[:, D_MODEL // 2:] << 16)
    rows = lax.bitcast_convert_type(words, jnp.int32).reshape(N_EXPERTS * HALF_ROWS, LANES)
    return jnp.pad(rows, ((TABLE_PAD, TABLE_PAD), (0, 0)))


def kernel(x, rel_bias, final_gain, attn_norm_gain, w_in, b_gate, attn_sinks, w_pool, pool_scale, gmlp_v_gain, w_spatial, b_spatial, w_branch_a, w_branch_b, w_branch_c, w_out, ffn_norm_gain, peer_w_query, peer_sub_keys1, peer_sub_keys2, peer_expert_u, peer_expert_v):
    B, S, D = x.shape
    T = B * S
    depth = w_in.shape[0]
    bf16 = jnp.bfloat16
    bias = _attention_bias(rel_bias)
    xt = x.reshape(T, D)
    for l in range(depth):
        q, kv, pz, guv, g = _in_proj(xt, attn_norm_gain[l][None], w_in[l].astype(bf16))
        xt = _mixer(xt, q, kv, pz, guv, g, attn_sinks[l], bias,
                    w_pool[l].astype(bf16), pool_scale[l][None], gmlp_v_gain[l][None],
                    w_spatial[l], b_spatial[l].T,
                    w_branch_a[l].astype(bf16), w_branch_b[l].astype(bf16),
                    w_branch_c[l].astype(bf16), w_out[l].astype(bf16), b_gate[l][None], B)
        h, expert, gate = _route(xt, ffn_norm_gain[l][None], peer_w_query[l].T.astype(bf16),
                                 peer_sub_keys1[l].astype(bf16), peer_sub_keys2[l].astype(bf16))
        expert = _by_slot(expert)
        coeff = _peer_u(expert, h.reshape(T, SUBLANES, LANES), gate, _pack_table(peer_expert_u[l]))
        xt = _peer_v(expert, _by_slot(coeff), xt.reshape(T, SUBLANES, LANES),
                     _pack_table(peer_expert_v[l])).reshape(T, D)
    return _final_norm(xt, final_gain[None]).reshape(B, S, D)
```

```python
import functools
import math

import jax
import jax.numpy as jnp
from jax import lax
from jax.experimental import pallas as pl
from jax.experimental.pallas import tpu as pltpu

D_MODEL = 1024
N_HEADS = 8
N_KV_HEADS = 2
HEAD_DIM = 64
GQA_GROUP = N_HEADS // N_KV_HEADS
WINDOW = 128
BLK = 128
Q_DIM = N_HEADS * HEAD_DIM
KV_DIM = N_KV_HEADS * HEAD_DIM
N_BUCKETS = 32
MAX_DISTANCE = 128
POOL_WINDOWS = (2, 4, 8, 16)
POOL_GROUP_DIM = 128
POOL_DIM = 512
POOL_TAIL = 16
GMLP_DIM = 512
GMLP_GROUPS = 4
GATE_DIM = 3 * D_MODEL
PEER_HEADS = 8
N_KEYS = 128
PEER_TOPK = 16
PEER_QDIM = 256
PEER_HALF = 128
PEER_SEL = PEER_HEADS * PEER_TOPK
EPS = 1e-6
NEG_INF = -1e30

SUBLANES = 8
LANES = 128
N_EXPERTS = N_KEYS * N_KEYS
HALF_ROWS = SUBLANES // 2
VMEM_LIMIT = 56 * 1024 * 1024


def _rmsnorm(x, gain):
    return x * lax.rsqrt(jnp.mean(x * x, axis=-1, keepdims=True) + EPS) * gain


def _gelu(x):
    c = math.sqrt(2.0 / math.pi)
    return x * (0.5 * (1.0 + jnp.tanh(c * (x + 0.044715 * (x * x * x)))))


def _dot(a, b):
    return jnp.dot(a, b, preferred_element_type=jnp.float32)


def _dot_nt(a, b):
    return lax.dot_general(a, b, (((1,), (1,)), ((), ())), preferred_element_type=jnp.float32)


IN_TM = 256


def _in_proj_kernel(x_ref, gain_ref, w_ref, q_ref, kv_ref, pz_ref, guv_ref, g_ref):
    h = _rmsnorm(x_ref[...], gain_ref[...]).astype(jnp.bfloat16)
    o = 0
    for ref in (q_ref, kv_ref, pz_ref, guv_ref, g_ref):
        n = ref.shape[1]
        ref[...] = _dot(h, w_ref[:, o:o + n]).astype(ref.dtype)
        o += n


def _in_proj(x, gain, w_in):
    T = x.shape[0]
    widths = (Q_DIM, 2 * KV_DIM, POOL_DIM, 2 * GMLP_DIM, GATE_DIM)
    dtypes = (jnp.bfloat16, jnp.bfloat16, jnp.float32, jnp.float32, jnp.float32)
    return pl.pallas_call(
        _in_proj_kernel,
        grid=(T // IN_TM,),
        in_specs=[
            pl.BlockSpec((IN_TM, D_MODEL), lambda i: (i, 0)),
            pl.BlockSpec((1, D_MODEL), lambda i: (0, 0)),
            pl.BlockSpec(w_in.shape, lambda i: (0, 0)),
        ],
        out_specs=[pl.BlockSpec((IN_TM, n), lambda i: (i, 0)) for n in widths],
        out_shape=[jax.ShapeDtypeStruct((T, n), d) for n, d in zip(widths, dtypes)],
        compiler_params=pltpu.CompilerParams(
            dimension_semantics=("parallel",), vmem_limit_bytes=VMEM_LIMIT),
        name="in_proj",
    )(x, gain, w_in)


def _mixer_kernel(sinks_ref, x_ref, q_ref, kvc_ref, kvp_ref, pzc_ref, pzp_ref, guv_ref, g_ref,
                  bias_ref, wpool_ref, pscale_ref, vgain_ref, wsp_ref, bsp_ref,
                  wa_ref, wb_ref, wc_ref, wout_ref, bgate_ref, o_ref):
    i = pl.program_id(1)
    first = i == 0

    kvc = kvc_ref[...]
    kvp = kvp_ref[...]
    kk = jnp.concatenate([kvp[:, :KV_DIM], kvc[:, :KV_DIM]], axis=0)
    vv = jnp.concatenate([kvp[:, KV_DIM:], kvc[:, KV_DIM:]], axis=0)
    qi = lax.broadcasted_iota(jnp.int32, (BLK, 2 * BLK), 0)
    kj = lax.broadcasted_iota(jnp.int32, (BLK, 2 * BLK), 1)
    dist = qi + BLK - kj
    visible = (dist >= 0) & (dist < WINDOW) & jnp.logical_not(first & (kj < BLK))
    q = q_ref[...]
    heads = []
    for h in range(N_HEADS):
        g = h // GQA_GROUP
        qh = q[:, h * HEAD_DIM:(h + 1) * HEAD_DIM]
        kh = kk[:, g * HEAD_DIM:(g + 1) * HEAD_DIM]
        vh = vv[:, g * HEAD_DIM:(g + 1) * HEAD_DIM]
        s = _dot_nt(qh, kh) * (HEAD_DIM ** -0.5) + bias_ref[h]
        s = jnp.where(visible, s, NEG_INF)
        sink = sinks_ref[h]
        m = jnp.maximum(jnp.max(s, axis=-1, keepdims=True), sink)
        p = jnp.exp(s - m)
        denom = jnp.sum(p, axis=-1, keepdims=True) + jnp.exp(sink - m)
        heads.append(_dot(p.astype(jnp.bfloat16), vh) / denom)
    attn = jnp.concatenate(heads, axis=-1).astype(jnp.bfloat16)
    ya = _dot(attn, wa_ref[...])

    zc = pzc_ref[...]
    zp = jnp.where(first, 0.0, pzp_ref[...])
    ext = jnp.concatenate([zp, zc], axis=0)
    pos = i * BLK + lax.broadcasted_iota(jnp.int32, (BLK, 1), 0)
    pooled = []
    for gi, w in enumerate(POOL_WINDOWS):
        lo = gi * POOL_GROUP_DIM
        s = ext[:, lo:lo + POOL_GROUP_DIM]
        width = 1
        while width < w:
            s = s + pltpu.roll(s, width, 0)
            width *= 2
        count = jnp.minimum(pos + 1, w).astype(jnp.float32)
        d = s[POOL_TAIL:] / count - zc[:, lo:lo + POOL_GROUP_DIM]
        pooled.append(_dot(d.astype(jnp.bfloat16), wpool_ref[gi]))
    pool = (jnp.concatenate(pooled, axis=-1) * pscale_ref[...]).astype(jnp.bfloat16)
    yb = _dot(pool, wb_ref[...])

    guv = guv_ref[...]
    u = _gelu(guv[:, :GMLP_DIM])
    v = _rmsnorm(_gelu(guv[:, GMLP_DIM:]), vgain_ref[...]).astype(jnp.bfloat16)
    ti = lax.broadcasted_iota(jnp.int32, (BLK, BLK), 0)
    si = lax.broadcasted_iota(jnp.int32, (BLK, BLK), 1)
    bsp = bsp_ref[...]
    mixed = []
    gd = GMLP_DIM // GMLP_GROUPS
    for gi in range(GMLP_GROUPS):
        w = jnp.where(si <= ti, wsp_ref[gi], 0.0).astype(jnp.bfloat16)
        mixed.append(_dot(w, v[:, gi * gd:(gi + 1) * gd]) + bsp[:, gi:gi + 1])
    gm = (u * jnp.concatenate(mixed, axis=-1)).astype(jnp.bfloat16)
    yc = _dot(gm, wc_ref[...])

    gates = jax.nn.sigmoid(g_ref[...] + bgate_ref[...])
    merged = (gates[:, :D_MODEL] * ya + gates[:, D_MODEL:2 * D_MODEL] * yb
              + gates[:, 2 * D_MODEL:] * yc)
    o_ref[...] = x_ref[...] + _dot(merged.astype(jnp.bfloat16), wout_ref[...])


def _mixer(x, q, kv, pz, guv, g, sinks, bias, wpool, pscale, vgain, wsp, bsp,
           wa, wb, wc, wout, bgate, batch):
    T = x.shape[0]
    nb = T // batch // BLK
    tail_per_blk = BLK // POOL_TAIL

    def cur(b, i):
        return (b * nb + i, 0)

    def prev(b, i):
        return (b * nb + jnp.maximum(i - 1, 0), 0)

    def prev_tail(b, i):
        return (jnp.maximum((b * nb + i) * tail_per_blk - 1, 0), 0)

    def const(shape):
        return pl.BlockSpec(shape, lambda b, i: (0,) * len(shape))

    return pl.pallas_call(
        _mixer_kernel,
        grid=(batch, nb),
        in_specs=[
            pl.BlockSpec(memory_space=pltpu.SMEM),
            pl.BlockSpec((BLK, D_MODEL), cur),
            pl.BlockSpec((BLK, Q_DIM), cur),
            pl.BlockSpec((BLK, 2 * KV_DIM), cur),
            pl.BlockSpec((BLK, 2 * KV_DIM), prev),
            pl.BlockSpec((BLK, POOL_DIM), cur),
            pl.BlockSpec((POOL_TAIL, POOL_DIM), prev_tail),
            pl.BlockSpec((BLK, 2 * GMLP_DIM), cur),
            pl.BlockSpec((BLK, GATE_DIM), cur),
            const(bias.shape), const(wpool.shape), const(pscale.shape), const(vgain.shape),
            const(wsp.shape), const(bsp.shape), const(wa.shape), const(wb.shape),
            const(wc.shape), const(wout.shape), const(bgate.shape),
        ],
        out_specs=pl.BlockSpec((BLK, D_MODEL), cur),
        out_shape=jax.ShapeDtypeStruct((T, D_MODEL), jnp.float32),
        compiler_params=pltpu.CompilerParams(
            dimension_semantics=("parallel", "parallel"), vmem_limit_bytes=VMEM_LIMIT),
        name="mixer",
    )(sinks, x, q, kv, kv, pz, pz, guv, g, bias, wpool, pscale, vgain, wsp, bsp,
      wa, wb, wc, wout, bgate)


def _topk_rows(s, k):
    rows = s.shape[0]
    rid = lax.broadcasted_iota(jnp.int32, s.shape, 0)
    vals, ids = [], []
    for _ in range(k):
        m = jnp.max(s, axis=0, keepdims=True)
        sel = jnp.min(jnp.where(s == m, rid, rows), axis=0, keepdims=True)
        vals.append(m)
        ids.append(sel)
        s = jnp.where(rid == sel, -jnp.inf, s)
    return jnp.concatenate(vals, axis=0), jnp.concatenate(ids, axis=0)


def _pick_rows(table, which, rows):
    out = jnp.zeros(which.shape, table.dtype)
    for r in range(rows):
        out = jnp.where(which == r, table[r:r + 1], out)
    return out


_CAND_WIDTHS = tuple(PEER_TOPK // (a + 1) for a in range(PEER_TOPK))
_CAND_PAD = -sum(_CAND_WIDTHS) % SUBLANES


def _route_kernel(x_ref, gain_ref, wq_ref, k1_ref, k2_ref, h_ref, expert_ref, gate_ref):
    h = _rmsnorm(x_ref[...], gain_ref[...])
    h_ref[...] = h
    qt = _dot_nt(wq_ref[...], h.astype(jnp.bfloat16)).astype(jnp.bfloat16)
    experts = []
    for hd in range(PEER_HEADS):
        base = hd * PEER_QDIM
        s1 = _dot(k1_ref[...], qt[base:base + PEER_HALF])
        s2 = _dot(k2_ref[...], qt[base + PEER_HALF:base + PEER_QDIM])
        v1, i1 = _topk_rows(s1, PEER_TOPK)
        v2, i2 = _topk_rows(s2, PEER_TOPK)
        cand = jnp.concatenate(
            [v1[a:a + 1] + v2[:w] for a, w in enumerate(_CAND_WIDTHS)]
            + [jnp.full((_CAND_PAD, BLK), -jnp.inf, jnp.float32)], axis=0)
        cand_expert = jnp.concatenate(
            [i1[a:a + 1] * N_KEYS + i2[:w] for a, w in enumerate(_CAND_WIDTHS)]
            + [jnp.zeros((_CAND_PAD, BLK), jnp.int32)], axis=0)
        top, ci = _topk_rows(cand, PEER_TOPK)
        rank = lax.broadcasted_iota(jnp.int32, (PEER_TOPK, BLK), 0)
        experts.append(_pick_rows(cand_expert, ci, sum(_CAND_WIDTHS)) * HALF_ROWS
                       + (TABLE_PAD - (rank & HALF_ROWS)))
        ex = jnp.exp(top - top[0:1])
        gate_ref[0, pl.ds(hd * PEER_TOPK, PEER_TOPK), :] = ex / jnp.sum(ex, axis=0, keepdims=True)
    expert_ref[0] = jnp.concatenate(experts, axis=0).T


def _route(x, gain, wq_t, keys1, keys2):
    T = x.shape[0]
    nblk = T // BLK
    sel_spec = pl.BlockSpec((1, PEER_SEL, BLK), lambda i: (i, 0, 0))
    tok_spec = pl.BlockSpec((1, BLK, PEER_SEL), lambda i: (i, 0, 0))
    return pl.pallas_call(
        _route_kernel,
        grid=(nblk,),
        in_specs=[
            pl.BlockSpec((BLK, D_MODEL), lambda i: (i, 0)),
            pl.BlockSpec((1, D_MODEL), lambda i: (0, 0)),
            pl.BlockSpec(wq_t.shape, lambda i: (0, 0)),
            pl.BlockSpec(keys1.shape, lambda i: (0, 0)),
            pl.BlockSpec(keys2.shape, lambda i: (0, 0)),
        ],
        out_specs=[pl.BlockSpec((BLK, D_MODEL), lambda i: (i, 0)), tok_spec, sel_spec],
        out_shape=[
            jax.ShapeDtypeStruct((T, D_MODEL), jnp.float32),
            jax.ShapeDtypeStruct((nblk, BLK, PEER_SEL), jnp.int32),
            jax.ShapeDtypeStruct((nblk, PEER_SEL, BLK), jnp.float32),
        ],
        compiler_params=pltpu.CompilerParams(
            dimension_semantics=("parallel",), vmem_limit_bytes=VMEM_LIMIT),
        name="route",
    )(x, gain, wq_t, keys1, keys2)


def _row_pair(tab_ref, wa, wb, first_half):
    words = jnp.where(first_half,
                      tab_ref[pl.ds(pl.multiple_of(wa, HALF_ROWS), SUBLANES), :],
                      tab_ref[pl.ds(pl.multiple_of(wb, HALF_ROWS), SUBLANES), :])
    lo = lax.bitcast_convert_type(lax.shift_left(words, 16), jnp.float32)
    hi = lax.bitcast_convert_type(words & jnp.int32(-65536), jnp.float32)
    return lo, hi


def _half_sums(q, sub):
    odd = (sub & 1) != 0
    c0 = jnp.where(odd, q[0] + pltpu.roll(q[0], 1, 0), q[1] + pltpu.roll(q[1], 7, 0))
    c1 = jnp.where(odd, q[2] + pltpu.roll(q[2], 1, 0), q[3] + pltpu.roll(q[3], 7, 0))
    return jnp.where((sub & 2) != 0, c0 + pltpu.roll(c0, 2, 0), c1 + pltpu.roll(c1, 6, 0))


TABLE_PAD = HALF_ROWS


N_GROUPS = PEER_SEL // SUBLANES
V_CHUNK_SHIFT = 0
V_CHUNK_GROUPS = N_GROUPS >> V_CHUNK_SHIFT


def _by_slot(a):
    nblk, _, width = a.shape
    groups = width // SUBLANES
    return (a.reshape(nblk, BLK, groups, SUBLANES).transpose(0, 3, 1, 2)
            .reshape(nblk, SUBLANES, BLK * groups))


def _slot_copies(src_hbm, dst_slots, sems, blk, half):
    n = src_hbm.shape[2]
    return [pltpu.make_async_copy(src_hbm.at[blk, j], dst_slots[j].at[pl.ds(half * n, n)],
                                  sems.at[half, j])
            for j in range(SUBLANES)]


def _fetch_slots(sources):
    i = pl.program_id(0)
    half = i % 2

    @pl.when(i == 0)
    def _():
        for src, dst, sems in sources:
            for copy in _slot_copies(src, dst, sems, 0, 0):
                copy.start()

    for src, dst, sems in sources:
        for copy in _slot_copies(src, dst, sems, i, half):
            copy.wait()

    @pl.when(i + 1 < pl.num_programs(0))
    def _():
        for src, dst, sems in sources:
            for copy in _slot_copies(src, dst, sems, i + 1, 1 - half):
                copy.start()

    return half


def _bf16_bits(x):
    b = lax.bitcast_convert_type(x, jnp.int32)
    return (b + 0x7FFF + ((b >> 16) & 1)) & jnp.int32(-65536)


def _peer_u_kernel(expert_hbm, h_ref, gate_ref, tab_ref, coeff_ref, *scratch):
    expert_s = scratch[:SUBLANES]
    sems, tiles_ref, act_ref = scratch[SUBLANES:]
    base = _fetch_slots([(expert_hbm, expert_s, sems)]) * (BLK * N_GROUPS)
    sub = lax.broadcasted_iota(jnp.int32, (SUBLANES, LANES), 0)
    first_half = sub < HALF_ROWS
    lane = lax.broadcasted_iota(jnp.int32, (PEER_SEL, BLK), 1)
    tiles_ref[...] = jnp.zeros(tiles_ref.shape, jnp.float32)
    act_ref[...] = jnp.zeros(act_ref.shape, jnp.float32)

    def reduce_token(t):
        src = pl.ds(pl.multiple_of((t & 1) * PEER_SEL, PEER_SEL), PEER_SEL)
        s = jnp.sum(tiles_ref[src, :], axis=1, keepdims=True)
        act_ref[...] = jnp.where(lane == t, s, act_ref[...])

    def token(t, carry):
        reduce_token(t - 1)
        ht = h_ref[t]
        h_lo = jnp.concatenate([ht[:HALF_ROWS], ht[:HALF_ROWS]], axis=0)
        h_hi = jnp.concatenate([ht[HALF_ROWS:], ht[HALF_ROWS:]], axis=0)
        for grp in range(N_GROUPS):
            r = base + t * N_GROUPS + grp
            q = []
            for i in reversed(range(HALF_ROWS)):
                lo, hi = _row_pair(tab_ref, expert_s[i][r], expert_s[i + HALF_ROWS][r], first_half)
                q.append(lo * h_lo + hi * h_hi)
            dst = (t & 1) * PEER_SEL + grp * SUBLANES
            tiles_ref[pl.ds(pl.multiple_of(dst, SUBLANES), SUBLANES), :] = _half_sums(q, sub)
        return carry

    lax.fori_loop(0, BLK, token, 0)
    reduce_token(BLK - 1)
    coeff = gate_ref[0] * _gelu(act_ref[...])
    partner = pltpu.roll(coeff, PEER_SEL - HALF_ROWS, 0)
    coeff_ref[0] = (_bf16_bits(partner) | lax.shift_right_logical(_bf16_bits(coeff), 16)).T


def _peer_v_kernel(expert_hbm, coeff_hbm, x_ref, tab_ref, o_ref, *scratch):
    expert_s, coeff_s = scratch[:SUBLANES], scratch[SUBLANES:2 * SUBLANES]
    expert_sems, coeff_sems = scratch[2 * SUBLANES:]
    base = _fetch_slots([(expert_hbm, expert_s, expert_sems),
                         (coeff_hbm, coeff_s, coeff_sems)]) * (BLK * N_GROUPS)
    first_half = lax.broadcasted_iota(jnp.int32, (SUBLANES, LANES), 0) < HALF_ROWS
    coeff_shift = jnp.where(first_half, 16, 0)
    o_ref[...] = x_ref[...]

    def chunk(n, carry):
        acc_lo = [jnp.zeros((SUBLANES, LANES), jnp.float32) for _ in range(2)]
        acc_hi = [jnp.zeros((SUBLANES, LANES), jnp.float32) for _ in range(2)]
        for grp in range(V_CHUNK_GROUPS):
            r = base + n * V_CHUNK_GROUPS + grp
            for i in range(HALF_ROWS):
                lo, hi = _row_pair(tab_ref, expert_s[i][r], expert_s[i + HALF_ROWS][r], first_half)
                word = jnp.full((SUBLANES, LANES), coeff_s[i][r], jnp.int32)
                c = lax.bitcast_convert_type(
                    lax.shift_left(word, coeff_shift) & jnp.int32(-65536), jnp.float32)
                acc_lo[i % 2] = acc_lo[i % 2] + c * lo
                acc_hi[i % 2] = acc_hi[i % 2] + c * hi
        y_lo = acc_lo[0] + acc_lo[1]
        y_hi = acc_hi[0] + acc_hi[1]
        y = jnp.concatenate([y_lo[:HALF_ROWS] + y_lo[HALF_ROWS:],
                             y_hi[:HALF_ROWS] + y_hi[HALF_ROWS:]], axis=0)
        t = n >> V_CHUNK_SHIFT
        o_ref[t] = o_ref[t] + y
        return carry

    lax.fori_loop(0, BLK << V_CHUNK_SHIFT, chunk, 0)


def _table_spec(tab):
    return pl.BlockSpec(tab.shape, lambda i: (0, 0), pipeline_mode=pl.Buffered(1))


def _slot_scratch(groups):
    return [pltpu.SMEM((2 * BLK * groups,), jnp.int32) for _ in range(SUBLANES)]


def _slot_sems():
    return pltpu.SemaphoreType.DMA((2, SUBLANES))


def _peer_u(expert, h3, gate, tab):
    nblk = expert.shape[0]
    return pl.pallas_call(
        _peer_u_kernel,
        grid=(nblk,),
        in_specs=[
            pl.BlockSpec(memory_space=pl.ANY),
            pl.BlockSpec((BLK, SUBLANES, LANES), lambda i: (i, 0, 0)),
            pl.BlockSpec((1, PEER_SEL, BLK), lambda i: (i, 0, 0)),
            _table_spec(tab),
        ],
        out_specs=pl.BlockSpec((1, BLK, PEER_SEL), lambda i: (i, 0, 0)),
        out_shape=jax.ShapeDtypeStruct((nblk, BLK, PEER_SEL), jnp.int32),
        scratch_shapes=_slot_scratch(N_GROUPS) + [
            _slot_sems(),
            pltpu.VMEM((2 * PEER_SEL, LANES), jnp.float32),
            pltpu.VMEM((PEER_SEL, BLK), jnp.float32),
        ],
        compiler_params=pltpu.CompilerParams(
            dimension_semantics=("arbitrary",), vmem_limit_bytes=VMEM_LIMIT),
        name="peer_u",
    )(expert, h3, gate, tab)


def _peer_v(expert, coeff, x3, tab):
    nblk = expert.shape[0]
    tok_spec = pl.BlockSpec((BLK, SUBLANES, LANES), lambda i: (i, 0, 0))
    return pl.pallas_call(
        _peer_v_kernel,
        grid=(nblk,),
        in_specs=[
            pl.BlockSpec(memory_space=pl.ANY),
            pl.BlockSpec(memory_space=pl.ANY),
            tok_spec,
            _table_spec(tab),
        ],
        out_specs=tok_spec,
        out_shape=jax.ShapeDtypeStruct(x3.shape, jnp.float32),
        scratch_shapes=(_slot_scratch(N_GROUPS) + _slot_scratch(N_GROUPS)
                        + [_slot_sems(), _slot_sems()]),
        compiler_params=pltpu.CompilerParams(
            dimension_semantics=("arbitrary",), vmem_limit_bytes=VMEM_LIMIT),
        name="peer_v",
    )(expert, coeff, x3, tab)


NORM_TM = 512


def _final_norm_kernel(x_ref, gain_ref, o_ref):
    o_ref[...] = _rmsnorm(x_ref[...], gain_ref[...])


def _final_norm(x, gain):
    T = x.shape[0]
    return pl.pallas_call(
        _final_norm_kernel,
        grid=(T // NORM_TM,),
        in_specs=[pl.BlockSpec((NORM_TM, D_MODEL), lambda i: (i, 0)),
                  pl.BlockSpec((1, D_MODEL), lambda i: (0, 0))],
        out_specs=pl.BlockSpec((NORM_TM, D_MODEL), lambda i: (i, 0)),
        out_shape=jax.ShapeDtypeStruct((T, D_MODEL), jnp.float32),
        compiler_params=pltpu.CompilerParams(dimension_semantics=("parallel",)),
        name="final_norm",
    )(x, gain)


def _t5_causal_bucket(dist):
    n = jnp.maximum(dist, 0)
    max_exact = N_BUCKETS // 2
    nf = jnp.maximum(n, 1).astype(jnp.float32)
    large = max_exact + (jnp.log(nf / max_exact) / math.log(MAX_DISTANCE / max_exact)
                         * (N_BUCKETS - max_exact)).astype(jnp.int32)
    large = jnp.minimum(large, N_BUCKETS - 1)
    return jnp.where(n < max_exact, n, large)


def _attention_bias(rel_bias):
    qi = jnp.arange(BLK)[:, None]
    kj = jnp.arange(2 * BLK)[None, :]
    bucket = _t5_causal_bucket(qi + BLK - kj)
    onehot = (bucket[:, :, None] == jnp.arange(N_BUCKETS)).astype(jnp.float32)
    return jnp.einsum('qkb,bh->hqk', onehot, rel_bias.astype(jnp.float32),
                      precision=lax.Precision.HIGHEST)


def _pack_table(tab):
    bits = _bf16_bits(tab)
    words = bits[:, D_MODEL // 2:] | lax.shift_right_logical(bits[:, :D_MODEL // 2], 16)
    pad = jnp.zeros((TABLE_PAD, LANES), jnp.int32)
    return jnp.concatenate([pad, words.reshape(N_EXPERTS * HALF_ROWS, LANES), pad], axis=0)


def kernel(x, rel_bias, final_gain, attn_norm_gain, w_in, b_gate, attn_sinks, w_pool, pool_scale, gmlp_v_gain, w_spatial, b_spatial, w_branch_a, w_branch_b, w_branch_c, w_out, ffn_norm_gain, peer_w_query, peer_sub_keys1, peer_sub_keys2, peer_expert_u, peer_expert_v):
    B, S, D = x.shape
    T = B * S
    depth = w_in.shape[0]
    bf16 = jnp.bfloat16
    bias = _attention_bias(rel_bias)
    xt = x.reshape(T, D)
    for l in range(depth):
        q, kv, pz, guv, g = _in_proj(xt, attn_norm_gain[l][None], w_in[l].astype(bf16))
        xt = _mixer(xt, q, kv, pz, guv, g, attn_sinks[l], bias,
                    w_pool[l].astype(bf16), pool_scale[l][None], gmlp_v_gain[l][None],
                    w_spatial[l], b_spatial[l].T,
                    w_branch_a[l].astype(bf16), w_branch_b[l].astype(bf16),
                    w_branch_c[l].astype(bf16), w_out[l].astype(bf16), b_gate[l][None], B)
        h, expert, gate = _route(xt, ffn_norm_gain[l][None], peer_w_query[l].T.astype(bf16),
                                 peer_sub_keys1[l].astype(bf16), peer_sub_keys2[l].astype(bf16))
        expert = _by_slot(expert)
        coeff = _peer_u(expert, h.reshape(T, SUBLANES, LANES), gate, _pack_table(peer_expert_u[l]))
        xt = _peer_v(expert, _by_slot(coeff), xt.reshape(T, SUBLANES, LANES),
                     _pack_table(peer_expert_v[l])).reshape(T, D)
    return _final_norm(xt, final_gain[None]).reshape(B, S, D)
```

```python
import functools
import math

import jax
import jax.numpy as jnp
from jax import lax
from jax.experimental import pallas as pl
from jax.experimental.pallas import tpu as pltpu

D_MODEL = 1024
N_HEADS = 8
N_KV_HEADS = 2
HEAD_DIM = 64
GQA_GROUP = N_HEADS // N_KV_HEADS
WINDOW = 128
BLK = 128
Q_DIM = N_HEADS * HEAD_DIM
KV_DIM = N_KV_HEADS * HEAD_DIM
N_BUCKETS = 32
MAX_DISTANCE = 128
POOL_WINDOWS = (2, 4, 8, 16)
POOL_GROUP_DIM = 128
POOL_DIM = 512
POOL_TAIL = 16
GMLP_DIM = 512
GMLP_GROUPS = 4
GATE_DIM = 3 * D_MODEL
PEER_HEADS = 8
N_KEYS = 128
PEER_TOPK = 16
PEER_QDIM = 256
PEER_HALF = 128
PEER_SEL = PEER_HEADS * PEER_TOPK
EPS = 1e-6
NEG_INF = -1e30

SUBLANES = 8
LANES = 128
N_EXPERTS = N_KEYS * N_KEYS
HALF_ROWS = SUBLANES // 2
VMEM_LIMIT = 56 * 1024 * 1024


def _rmsnorm(x, gain):
    return x * lax.rsqrt(jnp.mean(x * x, axis=-1, keepdims=True) + EPS) * gain


def _gelu(x):
    c = math.sqrt(2.0 / math.pi)
    return x * (0.5 * (1.0 + jnp.tanh(c * (x + 0.044715 * (x * x * x)))))


def _dot(a, b):
    return jnp.dot(a, b, preferred_element_type=jnp.float32)


def _dot_nt(a, b):
    return lax.dot_general(a, b, (((1,), (1,)), ((), ())), preferred_element_type=jnp.float32)


IN_TM = 256


def _in_proj_kernel(x_ref, gain_ref, w_ref, q_ref, kv_ref, pz_ref, guv_ref, g_ref):
    h = _rmsnorm(x_ref[...], gain_ref[...]).astype(jnp.bfloat16)
    o = 0
    for ref in (q_ref, kv_ref, pz_ref, guv_ref, g_ref):
        n = ref.shape[1]
        ref[...] = _dot(h, w_ref[:, o:o + n]).astype(ref.dtype)
        o += n


def _in_proj(x, gain, w_in):
    T = x.shape[0]
    widths = (Q_DIM, 2 * KV_DIM, POOL_DIM, 2 * GMLP_DIM, GATE_DIM)
    dtypes = (jnp.bfloat16, jnp.bfloat16, jnp.float32, jnp.float32, jnp.float32)
    return pl.pallas_call(
        _in_proj_kernel,
        grid=(T // IN_TM,),
        in_specs=[
            pl.BlockSpec((IN_TM, D_MODEL), lambda i: (i, 0)),
            pl.BlockSpec((1, D_MODEL), lambda i: (0, 0)),
            pl.BlockSpec(w_in.shape, lambda i: (0, 0)),
        ],
        out_specs=[pl.BlockSpec((IN_TM, n), lambda i: (i, 0)) for n in widths],
        out_shape=[jax.ShapeDtypeStruct((T, n), d) for n, d in zip(widths, dtypes)],
        compiler_params=pltpu.CompilerParams(
            dimension_semantics=("parallel",), vmem_limit_bytes=VMEM_LIMIT),
        name="in_proj",
    )(x, gain, w_in)


def _mixer_kernel(sinks_ref, x_ref, q_ref, kvc_ref, kvp_ref, pzc_ref, pzp_ref, guv_ref, g_ref,
                  bias_ref, wpool_ref, pscale_ref, vgain_ref, wsp_ref, bsp_ref,
                  wa_ref, wb_ref, wc_ref, wout_ref, bgate_ref, o_ref):
    i = pl.program_id(1)
    first = i == 0

    kvc = kvc_ref[...]
    kvp = kvp_ref[...]
    kk = jnp.concatenate([kvp[:, :KV_DIM], kvc[:, :KV_DIM]], axis=0)
    vv = jnp.concatenate([kvp[:, KV_DIM:], kvc[:, KV_DIM:]], axis=0)
    qi = lax.broadcasted_iota(jnp.int32, (BLK, 2 * BLK), 0)
    kj = lax.broadcasted_iota(jnp.int32, (BLK, 2 * BLK), 1)
    dist = qi + BLK - kj
    visible = (dist >= 0) & (dist < WINDOW) & jnp.logical_not(first & (kj < BLK))
    q = q_ref[...]
    heads = []
    for h in range(N_HEADS):
        g = h // GQA_GROUP
        qh = q[:, h * HEAD_DIM:(h + 1) * HEAD_DIM]
        kh = kk[:, g * HEAD_DIM:(g + 1) * HEAD_DIM]
        vh = vv[:, g * HEAD_DIM:(g + 1) * HEAD_DIM]
        s = _dot_nt(qh, kh) * (HEAD_DIM ** -0.5) + bias_ref[h]
        s = jnp.where(visible, s, NEG_INF)
        sink = sinks_ref[h]
        m = jnp.maximum(jnp.max(s, axis=-1, keepdims=True), sink)
        p = jnp.exp(s - m)
        denom = jnp.sum(p, axis=-1, keepdims=True) + jnp.exp(sink - m)
        heads.append(_dot(p.astype(jnp.bfloat16), vh) / denom)
    attn = jnp.concatenate(heads, axis=-1).astype(jnp.bfloat16)
    ya = _dot(attn, wa_ref[...])

    zc = pzc_ref[...]
    zp = jnp.where(first, 0.0, pzp_ref[...])
    ext = jnp.concatenate([zp, zc], axis=0)
    pos = i * BLK + lax.broadcasted_iota(jnp.int32, (BLK, 1), 0)
    pooled = []
    for gi, w in enumerate(POOL_WINDOWS):
        lo = gi * POOL_GROUP_DIM
        s = ext[:, lo:lo + POOL_GROUP_DIM]
        width = 1
        while width < w:
            s = s + pltpu.roll(s, width, 0)
            width *= 2
        count = jnp.minimum(pos + 1, w).astype(jnp.float32)
        d = s[POOL_TAIL:] / count - zc[:, lo:lo + POOL_GROUP_DIM]
        pooled.append(_dot(d.astype(jnp.bfloat16), wpool_ref[gi]))
    pool = (jnp.concatenate(pooled, axis=-1) * pscale_ref[...]).astype(jnp.bfloat16)
    yb = _dot(pool, wb_ref[...])

    guv = guv_ref[...]
    u = _gelu(guv[:, :GMLP_DIM])
    v = _rmsnorm(_gelu(guv[:, GMLP_DIM:]), vgain_ref[...]).astype(jnp.bfloat16)
    ti = lax.broadcasted_iota(jnp.int32, (BLK, BLK), 0)
    si = lax.broadcasted_iota(jnp.int32, (BLK, BLK), 1)
    bsp = bsp_ref[...]
    mixed = []
    gd = GMLP_DIM // GMLP_GROUPS
    for gi in range(GMLP_GROUPS):
        w = jnp.where(si <= ti, wsp_ref[gi], 0.0).astype(jnp.bfloat16)
        mixed.append(_dot(w, v[:, gi * gd:(gi + 1) * gd]) + bsp[:, gi:gi + 1])
    gm = (u * jnp.concatenate(mixed, axis=-1)).astype(jnp.bfloat16)
    yc = _dot(gm, wc_ref[...])

    gates = jax.nn.sigmoid(g_ref[...] + bgate_ref[...])
    merged = (gates[:, :D_MODEL] * ya + gates[:, D_MODEL:2 * D_MODEL] * yb
              + gates[:, 2 * D_MODEL:] * yc)
    o_ref[...] = x_ref[...] + _dot(merged.astype(jnp.bfloat16), wout_ref[...])


def _mixer(x, q, kv, pz, guv, g, sinks, bias, wpool, pscale, vgain, wsp, bsp,
           wa, wb, wc, wout, bgate, batch):
    T = x.shape[0]
    nb = T // batch // BLK
    tail_per_blk = BLK // POOL_TAIL

    def cur(b, i):
        return (b * nb + i, 0)

    def prev(b, i):
        return (b * nb + jnp.maximum(i - 1, 0), 0)

    def prev_tail(b, i):
        return (jnp.maximum((b * nb + i) * tail_per_blk - 1, 0), 0)

    def const(shape):
        return pl.BlockSpec(shape, lambda b, i: (0,) * len(shape))

    return pl.pallas_call(
        _mixer_kernel,
        grid=(batch, nb),
        in_specs=[
            pl.BlockSpec(memory_space=pltpu.SMEM),
            pl.BlockSpec((BLK, D_MODEL), cur),
            pl.BlockSpec((BLK, Q_DIM), cur),
            pl.BlockSpec((BLK, 2 * KV_DIM), cur),
            pl.BlockSpec((BLK, 2 * KV_DIM), prev),
            pl.BlockSpec((BLK, POOL_DIM), cur),
            pl.BlockSpec((POOL_TAIL, POOL_DIM), prev_tail),
            pl.BlockSpec((BLK, 2 * GMLP_DIM), cur),
            pl.BlockSpec((BLK, GATE_DIM), cur),
            const(bias.shape), const(wpool.shape), const(pscale.shape), const(vgain.shape),
            const(wsp.shape), const(bsp.shape), const(wa.shape), const(wb.shape),
            const(wc.shape), const(wout.shape), const(bgate.shape),
        ],
        out_specs=pl.BlockSpec((BLK, D_MODEL), cur),
        out_shape=jax.ShapeDtypeStruct((T, D_MODEL), jnp.float32),
        compiler_params=pltpu.CompilerParams(
            dimension_semantics=("parallel", "parallel"), vmem_limit_bytes=VMEM_LIMIT),
        name="mixer",
    )(sinks, x, q, kv, kv, pz, pz, guv, g, bias, wpool, pscale, vgain, wsp, bsp,
      wa, wb, wc, wout, bgate)


def _topk_rows(s, k):
    rows = s.shape[0]
    rid = lax.broadcasted_iota(jnp.int32, s.shape, 0)
    vals, ids = [], []
    for _ in range(k):
        m = jnp.max(s, axis=0, keepdims=True)
        sel = jnp.min(jnp.where(s == m, rid, rows), axis=0, keepdims=True)
        vals.append(m)
        ids.append(sel)
        s = jnp.where(rid == sel, -jnp.inf, s)
    return jnp.concatenate(vals, axis=0), jnp.concatenate(ids, axis=0)


def _pick_rows(table, which, rows):
    out = jnp.zeros(which.shape, table.dtype)
    for r in range(rows):
        out = jnp.where(which == r, table[r:r + 1], out)
    return out


_CAND_WIDTHS = tuple(PEER_TOPK // (a + 1) for a in range(PEER_TOPK))
_CAND_PAD = -sum(_CAND_WIDTHS) % SUBLANES


def _route_kernel(x_ref, gain_ref, wq_ref, k1_ref, k2_ref, h_ref, expert_ref, gate_ref):
    h = _rmsnorm(x_ref[...], gain_ref[...])
    h_ref[...] = h
    qt = _dot_nt(wq_ref[...], h.astype(jnp.bfloat16)).astype(jnp.bfloat16)
    experts = []
    for hd in range(PEER_HEADS):
        base = hd * PEER_QDIM
        s1 = _dot(k1_ref[...], qt[base:base + PEER_HALF])
        s2 = _dot(k2_ref[...], qt[base + PEER_HALF:base + PEER_QDIM])
        v1, i1 = _topk_rows(s1, PEER_TOPK)
        v2, i2 = _topk_rows(s2, PEER_TOPK)
        cand = jnp.concatenate(
            [v1[a:a + 1] + v2[:w] for a, w in enumerate(_CAND_WIDTHS)]
            + [jnp.full((_CAND_PAD, BLK), -jnp.inf, jnp.float32)], axis=0)
        cand_expert = jnp.concatenate(
            [i1[a:a + 1] * N_KEYS + i2[:w] for a, w in enumerate(_CAND_WIDTHS)]
            + [jnp.zeros((_CAND_PAD, BLK), jnp.int32)], axis=0)
        top, ci = _topk_rows(cand, PEER_TOPK)
        rank = lax.broadcasted_iota(jnp.int32, (PEER_TOPK, BLK), 0)
        experts.append(_pick_rows(cand_expert, ci, sum(_CAND_WIDTHS)) * HALF_ROWS
                       + (TABLE_PAD - (rank & HALF_ROWS)))
        ex = jnp.exp(top - top[0:1])
        gate_ref[0, pl.ds(hd * PEER_TOPK, PEER_TOPK), :] = ex / jnp.sum(ex, axis=0, keepdims=True)
    expert_ref[0] = jnp.concatenate(experts, axis=0).T


def _route(x, gain, wq_t, keys1, keys2):
    T = x.shape[0]
    nblk = T // BLK
    sel_spec = pl.BlockSpec((1, PEER_SEL, BLK), lambda i: (i, 0, 0))
    tok_spec = pl.BlockSpec((1, BLK, PEER_SEL), lambda i: (i, 0, 0))
    return pl.pallas_call(
        _route_kernel,
        grid=(nblk,),
        in_specs=[
            pl.BlockSpec((BLK, D_MODEL), lambda i: (i, 0)),
            pl.BlockSpec((1, D_MODEL), lambda i: (0, 0)),
            pl.BlockSpec(wq_t.shape, lambda i: (0, 0)),
            pl.BlockSpec(keys1.shape, lambda i: (0, 0)),
            pl.BlockSpec(keys2.shape, lambda i: (0, 0)),
        ],
        out_specs=[pl.BlockSpec((BLK, D_MODEL), lambda i: (i, 0)), tok_spec, sel_spec],
        out_shape=[
            jax.ShapeDtypeStruct((T, D_MODEL), jnp.float32),
            jax.ShapeDtypeStruct((nblk, BLK, PEER_SEL), jnp.int32),
            jax.ShapeDtypeStruct((nblk, PEER_SEL, BLK), jnp.float32),
        ],
        compiler_params=pltpu.CompilerParams(
            dimension_semantics=("parallel",), vmem_limit_bytes=VMEM_LIMIT),
        name="route",
    )(x, gain, wq_t, keys1, keys2)


def _row_pair(tab_ref, wa, wb, first_half):
    words = jnp.where(first_half,
                      tab_ref[pl.ds(pl.multiple_of(wa, HALF_ROWS), SUBLANES), :],
                      tab_ref[pl.ds(pl.multiple_of(wb, HALF_ROWS), SUBLANES), :])
    lo = lax.bitcast_convert_type(lax.shift_left(words, 16), jnp.float32)
    hi = lax.bitcast_convert_type(words & jnp.int32(-65536), jnp.float32)
    return lo, hi


def _half_sums(q, sub):
    odd = (sub & 1) != 0
    c0 = jnp.where(odd, q[0] + pltpu.roll(q[0], 1, 0), q[1] + pltpu.roll(q[1], 7, 0))
    c1 = jnp.where(odd, q[2] + pltpu.roll(q[2], 1, 0), q[3] + pltpu.roll(q[3], 7, 0))
    return jnp.where((sub & 2) != 0, c0 + pltpu.roll(c0, 2, 0), c1 + pltpu.roll(c1, 6, 0))


PACK_TM = 256
TABLE_PAD = PACK_TM * HALF_ROWS


N_GROUPS = PEER_SEL // SUBLANES
V_CHUNK_SHIFT = 0
V_CHUNK_GROUPS = N_GROUPS >> V_CHUNK_SHIFT


def _by_slot(a):
    nblk, _, width = a.shape
    groups = width // SUBLANES
    return (a.reshape(nblk, BLK, groups, SUBLANES).transpose(0, 3, 1, 2)
            .reshape(nblk, SUBLANES, BLK * groups))


def _slot_copies(src_hbm, dst_slots, sems, blk, half):
    n = src_hbm.shape[2]
    return [pltpu.make_async_copy(src_hbm.at[blk, j], dst_slots[j].at[pl.ds(half * n, n)],
                                  sems.at[half, j])
            for j in range(SUBLANES)]


def _fetch_slots(sources):
    i = pl.program_id(0)
    half = i % 2

    @pl.when(i == 0)
    def _():
        for src, dst, sems in sources:
            for copy in _slot_copies(src, dst, sems, 0, 0):
                copy.start()

    for src, dst, sems in sources:
        for copy in _slot_copies(src, dst, sems, i, half):
            copy.wait()

    @pl.when(i + 1 < pl.num_programs(0))
    def _():
        for src, dst, sems in sources:
            for copy in _slot_copies(src, dst, sems, i + 1, 1 - half):
                copy.start()

    return half


def _bf16_bits(x):
    b = lax.bitcast_convert_type(x, jnp.int32)
    return (b + 0x7FFF + ((b >> 16) & 1)) & jnp.int32(-65536)


def _peer_u_kernel(expert_hbm, h_ref, gate_ref, tab_ref, coeff_ref, *scratch):
    expert_s = scratch[:SUBLANES]
    sems, tiles_ref, act_ref = scratch[SUBLANES:]
    base = _fetch_slots([(expert_hbm, expert_s, sems)]) * (BLK * N_GROUPS)
    sub = lax.broadcasted_iota(jnp.int32, (SUBLANES, LANES), 0)
    first_half = sub < HALF_ROWS
    lane = lax.broadcasted_iota(jnp.int32, (PEER_SEL, BLK), 1)
    tiles_ref[...] = jnp.zeros(tiles_ref.shape, jnp.float32)
    act_ref[...] = jnp.zeros(act_ref.shape, jnp.float32)

    def reduce_token(t):
        src = pl.ds(pl.multiple_of((t & 1) * PEER_SEL, PEER_SEL), PEER_SEL)
        s = jnp.sum(tiles_ref[src, :], axis=1, keepdims=True)
        act_ref[...] = jnp.where(lane == t, s, act_ref[...])

    def token(t, carry):
        reduce_token(t - 1)
        ht = h_ref[t]
        h_lo = jnp.concatenate([ht[:HALF_ROWS], ht[:HALF_ROWS]], axis=0)
        h_hi = jnp.concatenate([ht[HALF_ROWS:], ht[HALF_ROWS:]], axis=0)
        for grp in range(N_GROUPS):
            r = base + t * N_GROUPS + grp
            q = []
            for i in reversed(range(HALF_ROWS)):
                lo, hi = _row_pair(tab_ref, expert_s[i][r], expert_s[i + HALF_ROWS][r], first_half)
                q.append(lo * h_lo + hi * h_hi)
            dst = (t & 1) * PEER_SEL + grp * SUBLANES
            tiles_ref[pl.ds(pl.multiple_of(dst, SUBLANES), SUBLANES), :] = _half_sums(q, sub)
        return carry

    lax.fori_loop(0, BLK, token, 0)
    reduce_token(BLK - 1)
    coeff = gate_ref[0] * _gelu(act_ref[...])
    partner = pltpu.roll(coeff, PEER_SEL - HALF_ROWS, 0)
    coeff_ref[0] = (_bf16_bits(partner) | lax.shift_right_logical(_bf16_bits(coeff), 16)).T


def _peer_v_kernel(expert_hbm, coeff_hbm, x_ref, tab_ref, o_ref, *scratch):
    expert_s, coeff_s = scratch[:SUBLANES], scratch[SUBLANES:2 * SUBLANES]
    expert_sems, coeff_sems = scratch[2 * SUBLANES:]
    base = _fetch_slots([(expert_hbm, expert_s, expert_sems),
                         (coeff_hbm, coeff_s, coeff_sems)]) * (BLK * N_GROUPS)
    first_half = lax.broadcasted_iota(jnp.int32, (SUBLANES, LANES), 0) < HALF_ROWS
    coeff_shift = jnp.where(first_half, 16, 0)
    o_ref[...] = x_ref[...]

    def chunk(n, carry):
        acc_lo = [jnp.zeros((SUBLANES, LANES), jnp.float32) for _ in range(2)]
        acc_hi = [jnp.zeros((SUBLANES, LANES), jnp.float32) for _ in range(2)]
        for grp in range(V_CHUNK_GROUPS):
            r = base + n * V_CHUNK_GROUPS + grp
            for i in range(HALF_ROWS):
                lo, hi = _row_pair(tab_ref, expert_s[i][r], expert_s[i + HALF_ROWS][r], first_half)
                word = jnp.full((SUBLANES, LANES), coeff_s[i][r], jnp.int32)
                c = lax.bitcast_convert_type(
                    lax.shift_left(word, coeff_shift) & jnp.int32(-65536), jnp.float32)
                acc_lo[i % 2] = acc_lo[i % 2] + c * lo
                acc_hi[i % 2] = acc_hi[i % 2] + c * hi
        y_lo = acc_lo[0] + acc_lo[1]
        y_hi = acc_hi[0] + acc_hi[1]
        y = jnp.concatenate([y_lo[:HALF_ROWS] + y_lo[HALF_ROWS:],
                             y_hi[:HALF_ROWS] + y_hi[HALF_ROWS:]], axis=0)
        t = n >> V_CHUNK_SHIFT
        o_ref[t] = o_ref[t] + y
        return carry

    lax.fori_loop(0, BLK << V_CHUNK_SHIFT, chunk, 0)


def _table_spec(tab):
    return pl.BlockSpec(tab.shape, lambda i: (0, 0), pipeline_mode=pl.Buffered(1))


def _slot_scratch(groups):
    return [pltpu.SMEM((2 * BLK * groups,), jnp.int32) for _ in range(SUBLANES)]


def _slot_sems():
    return pltpu.SemaphoreType.DMA((2, SUBLANES))


def _peer_u(expert, h3, gate, tab):
    nblk = expert.shape[0]
    return pl.pallas_call(
        _peer_u_kernel,
        grid=(nblk,),
        in_specs=[
            pl.BlockSpec(memory_space=pl.ANY),
            pl.BlockSpec((BLK, SUBLANES, LANES), lambda i: (i, 0, 0)),
            pl.BlockSpec((1, PEER_SEL, BLK), lambda i: (i, 0, 0)),
            _table_spec(tab),
        ],
        out_specs=pl.BlockSpec((1, BLK, PEER_SEL), lambda i: (i, 0, 0)),
        out_shape=jax.ShapeDtypeStruct((nblk, BLK, PEER_SEL), jnp.int32),
        scratch_shapes=_slot_scratch(N_GROUPS) + [
            _slot_sems(),
            pltpu.VMEM((2 * PEER_SEL, LANES), jnp.float32),
            pltpu.VMEM((PEER_SEL, BLK), jnp.float32),
        ],
        compiler_params=pltpu.CompilerParams(
            dimension_semantics=("arbitrary",), vmem_limit_bytes=VMEM_LIMIT),
        name="peer_u",
    )(expert, h3, gate, tab)


def _peer_v(expert, coeff, x3, tab):
    nblk = expert.shape[0]
    tok_spec = pl.BlockSpec((BLK, SUBLANES, LANES), lambda i: (i, 0, 0))
    return pl.pallas_call(
        _peer_v_kernel,
        grid=(nblk,),
        in_specs=[
            pl.BlockSpec(memory_space=pl.ANY),
            pl.BlockSpec(memory_space=pl.ANY),
            tok_spec,
            _table_spec(tab),
        ],
        out_specs=tok_spec,
        out_shape=jax.ShapeDtypeStruct(x3.shape, jnp.float32),
        scratch_shapes=(_slot_scratch(N_GROUPS) + _slot_scratch(N_GROUPS)
                        + [_slot_sems(), _slot_sems()]),
        compiler_params=pltpu.CompilerParams(
            dimension_semantics=("arbitrary",), vmem_limit_bytes=VMEM_LIMIT),
        name="peer_v",
    )(expert, coeff, x3, tab)


NORM_TM = 512


def _final_norm_kernel(x_ref, gain_ref, o_ref):
    o_ref[...] = _rmsnorm(x_ref[...], gain_ref[...])


def _final_norm(x, gain):
    T = x.shape[0]
    return pl.pallas_call(
        _final_norm_kernel,
        grid=(T // NORM_TM,),
        in_specs=[pl.BlockSpec((NORM_TM, D_MODEL), lambda i: (i, 0)),
                  pl.BlockSpec((1, D_MODEL), lambda i: (0, 0))],
        out_specs=pl.BlockSpec((NORM_TM, D_MODEL), lambda i: (i, 0)),
        out_shape=jax.ShapeDtypeStruct((T, D_MODEL), jnp.float32),
        compiler_params=pltpu.CompilerParams(dimension_semantics=("parallel",)),
        name="final_norm",
    )(x, gain)


def _t5_causal_bucket(dist):
    n = jnp.maximum(dist, 0)
    max_exact = N_BUCKETS // 2
    nf = jnp.maximum(n, 1).astype(jnp.float32)
    large = max_exact + (jnp.log(nf / max_exact) / math.log(MAX_DISTANCE / max_exact)
                         * (N_BUCKETS - max_exact)).astype(jnp.int32)
    large = jnp.minimum(large, N_BUCKETS - 1)
    return jnp.where(n < max_exact, n, large)


def _attention_bias(rel_bias):
    qi = jnp.arange(BLK)[:, None]
    kj = jnp.arange(2 * BLK)[None, :]
    bucket = _t5_causal_bucket(qi + BLK - kj)
    onehot = (bucket[:, :, None] == jnp.arange(N_BUCKETS)).astype(jnp.float32)
    return jnp.einsum('qkb,bh->hqk', onehot, rel_bias.astype(jnp.float32),
                      precision=lax.Precision.HIGHEST)


def _pack_table(tab):
    steps = N_EXPERTS // PACK_TM + 2
    return pl.pallas_call(
        _pack_kernel,
        grid=(steps,),
        in_specs=[pl.BlockSpec((PACK_TM, D_MODEL),
                               lambda i: (jnp.clip(i - 1, 0, N_EXPERTS // PACK_TM - 1), 0))],
        out_specs=pl.BlockSpec((TABLE_PAD, LANES), lambda i: (i, 0)),
        out_shape=jax.ShapeDtypeStruct((steps * TABLE_PAD, LANES), jnp.int32),
        compiler_params=pltpu.CompilerParams(dimension_semantics=("parallel",)),
        name="pack_table",
    )(tab)


def _pack_kernel(tab_ref, out_ref):
    i = pl.program_id(0)
    interior = (i > 0) & (i < pl.num_programs(0) - 1)

    @pl.when(interior)
    def _():
        bits = _bf16_bits(tab_ref[...])
        words = bits[:, D_MODEL // 2:] | lax.shift_right_logical(bits[:, :D_MODEL // 2], 16)
        for s in range(HALF_ROWS):
            out_ref[pl.ds(s, PACK_TM, stride=HALF_ROWS), :] = words[:, s * LANES:(s + 1) * LANES]

    @pl.when(jnp.logical_not(interior))
    def _():
        out_ref[...] = jnp.zeros(out_ref.shape, jnp.int32)


def kernel(x, rel_bias, final_gain, attn_norm_gain, w_in, b_gate, attn_sinks, w_pool, pool_scale, gmlp_v_gain, w_spatial, b_spatial, w_branch_a, w_branch_b, w_branch_c, w_out, ffn_norm_gain, peer_w_query, peer_sub_keys1, peer_sub_keys2, peer_expert_u, peer_expert_v):
    B, S, D = x.shape
    T = B * S
    depth = w_in.shape[0]
    bf16 = jnp.bfloat16
    bias = _attention_bias(rel_bias)
    xt = x.reshape(T, D)
    for l in range(depth):
        q, kv, pz, guv, g = _in_proj(xt, attn_norm_gain[l][None], w_in[l].astype(bf16))
        xt = _mixer(xt, q, kv, pz, guv, g, attn_sinks[l], bias,
                    w_pool[l].astype(bf16), pool_scale[l][None], gmlp_v_gain[l][None],
                    w_spatial[l], b_spatial[l].T,
                    w_branch_a[l].astype(bf16), w_branch_b[l].astype(bf16),
                    w_branch_c[l].astype(bf16), w_out[l].astype(bf16), b_gate[l][None], B)
        h, expert, gate = _route(xt, ffn_norm_gain[l][None], peer_w_query[l].T.astype(bf16),
                                 peer_sub_keys1[l].astype(bf16), peer_sub_keys2[l].astype(bf16))
        expert = _by_slot(expert)
        coeff = _peer_u(expert, h.reshape(T, SUBLANES, LANES), gate, _pack_table(peer_expert_u[l]))
        xt = _peer_v(expert, _by_slot(coeff), xt.reshape(T, SUBLANES, LANES),
                     _pack_table(peer_expert_v[l])).reshape(T, D)
    return _final_norm(xt, final_gain[None]).reshape(B, S, D)
```

```python
import functools
import math

import jax
import jax.numpy as jnp
from jax import lax
from jax.experimental import pallas as pl
from jax.experimental.pallas import tpu as pltpu

D_MODEL = 1024
N_HEADS = 8
N_KV_HEADS = 2
HEAD_DIM = 64
GQA_GROUP = N_HEADS // N_KV_HEADS
WINDOW = 128
BLK = 128
Q_DIM = N_HEADS * HEAD_DIM
KV_DIM = N_KV_HEADS * HEAD_DIM
N_BUCKETS = 32
MAX_DISTANCE = 128
POOL_WINDOWS = (2, 4, 8, 16)
POOL_GROUP_DIM = 128
POOL_DIM = 512
POOL_TAIL = 16
GMLP_DIM = 512
GMLP_GROUPS = 4
GATE_DIM = 3 * D_MODEL
PEER_HEADS = 8
N_KEYS = 128
PEER_TOPK = 16
PEER_QDIM = 256
PEER_HALF = 128
PEER_SEL = PEER_HEADS * PEER_TOPK
EPS = 1e-6
NEG_INF = -1e30

SUBLANES = 8
LANES = 128
N_EXPERTS = N_KEYS * N_KEYS
HALF_ROWS = SUBLANES // 2
VMEM_LIMIT = 56 * 1024 * 1024


def _rmsnorm(x, gain):
    return x * lax.rsqrt(jnp.mean(x * x, axis=-1, keepdims=True) + EPS) * gain


def _gelu(x):
    c = math.sqrt(2.0 / math.pi)
    return x * (0.5 * (1.0 + jnp.tanh(c * (x + 0.044715 * (x * x * x)))))


def _dot(a, b):
    return jnp.dot(a, b, preferred_element_type=jnp.float32)


def _dot_nt(a, b):
    return lax.dot_general(a, b, (((1,), (1,)), ((), ())), preferred_element_type=jnp.float32)


IN_TM = 256


def _in_proj_kernel(x_ref, gain_ref, w_ref, q_ref, kv_ref, pz_ref, guv_ref, g_ref):
    h = _rmsnorm(x_ref[...], gain_ref[...]).astype(jnp.bfloat16)
    o = 0
    for ref in (q_ref, kv_ref, pz_ref, guv_ref, g_ref):
        n = ref.shape[1]
        ref[...] = _dot(h, w_ref[:, o:o + n]).astype(ref.dtype)
        o += n


def _in_proj(x, gain, w_in):
    T = x.shape[0]
    widths = (Q_DIM, 2 * KV_DIM, POOL_DIM, 2 * GMLP_DIM, GATE_DIM)
    dtypes = (jnp.bfloat16, jnp.bfloat16, jnp.float32, jnp.float32, jnp.float32)
    return pl.pallas_call(
        _in_proj_kernel,
        grid=(T // IN_TM,),
        in_specs=[
            pl.BlockSpec((IN_TM, D_MODEL), lambda i: (i, 0)),
            pl.BlockSpec((1, D_MODEL), lambda i: (0, 0)),
            pl.BlockSpec(w_in.shape, lambda i: (0, 0)),
        ],
        out_specs=[pl.BlockSpec((IN_TM, n), lambda i: (i, 0)) for n in widths],
        out_shape=[jax.ShapeDtypeStruct((T, n), d) for n, d in zip(widths, dtypes)],
        compiler_params=pltpu.CompilerParams(
            dimension_semantics=("parallel",), vmem_limit_bytes=VMEM_LIMIT),
        name="in_proj",
    )(x, gain, w_in)


def _mixer_kernel(sinks_ref, x_ref, q_ref, kvc_ref, kvp_ref, pzc_ref, pzp_ref, guv_ref, g_ref,
                  bias_ref, wpool_ref, pscale_ref, vgain_ref, wsp_ref, bsp_ref,
                  wa_ref, wb_ref, wc_ref, wout_ref, bgate_ref, o_ref):
    i = pl.program_id(1)
    first = i == 0

    kvc = kvc_ref[...]
    kvp = kvp_ref[...]
    kk = jnp.concatenate([kvp[:, :KV_DIM], kvc[:, :KV_DIM]], axis=0)
    vv = jnp.concatenate([kvp[:, KV_DIM:], kvc[:, KV_DIM:]], axis=0)
    qi = lax.broadcasted_iota(jnp.int32, (BLK, 2 * BLK), 0)
    kj = lax.broadcasted_iota(jnp.int32, (BLK, 2 * BLK), 1)
    dist = qi + BLK - kj
    visible = (dist >= 0) & (dist < WINDOW) & jnp.logical_not(first & (kj < BLK))
    q = q_ref[...]
    heads = []
    for h in range(N_HEADS):
        g = h // GQA_GROUP
        qh = q[:, h * HEAD_DIM:(h + 1) * HEAD_DIM]
        kh = kk[:, g * HEAD_DIM:(g + 1) * HEAD_DIM]
        vh = vv[:, g * HEAD_DIM:(g + 1) * HEAD_DIM]
        s = _dot_nt(qh, kh) * (HEAD_DIM ** -0.5) + bias_ref[h]
        s = jnp.where(visible, s, NEG_INF)
        sink = sinks_ref[h]
        m = jnp.maximum(jnp.max(s, axis=-1, keepdims=True), sink)
        p = jnp.exp(s - m)
        denom = jnp.sum(p, axis=-1, keepdims=True) + jnp.exp(sink - m)
        heads.append(_dot(p.astype(jnp.bfloat16), vh) / denom)
    attn = jnp.concatenate(heads, axis=-1).astype(jnp.bfloat16)
    ya = _dot(attn, wa_ref[...])

    zc = pzc_ref[...]
    zp = jnp.where(first, 0.0, pzp_ref[...])
    ext = jnp.concatenate([zp, zc], axis=0)
    pos = i * BLK + lax.broadcasted_iota(jnp.int32, (BLK, 1), 0)
    pooled = []
    for gi, w in enumerate(POOL_WINDOWS):
        lo = gi * POOL_GROUP_DIM
        s = ext[:, lo:lo + POOL_GROUP_DIM]
        width = 1
        while width < w:
            s = s + pltpu.roll(s, width, 0)
            width *= 2
        count = jnp.minimum(pos + 1, w).astype(jnp.float32)
        d = s[POOL_TAIL:] / count - zc[:, lo:lo + POOL_GROUP_DIM]
        pooled.append(_dot(d.astype(jnp.bfloat16), wpool_ref[gi]))
    pool = (jnp.concatenate(pooled, axis=-1) * pscale_ref[...]).astype(jnp.bfloat16)
    yb = _dot(pool, wb_ref[...])

    guv = guv_ref[...]
    u = _gelu(guv[:, :GMLP_DIM])
    v = _rmsnorm(_gelu(guv[:, GMLP_DIM:]), vgain_ref[...]).astype(jnp.bfloat16)
    ti = lax.broadcasted_iota(jnp.int32, (BLK, BLK), 0)
    si = lax.broadcasted_iota(jnp.int32, (BLK, BLK), 1)
    bsp = bsp_ref[...]
    mixed = []
    gd = GMLP_DIM // GMLP_GROUPS
    for gi in range(GMLP_GROUPS):
        w = jnp.where(si <= ti, wsp_ref[gi], 0.0).astype(jnp.bfloat16)
        mixed.append(_dot(w, v[:, gi * gd:(gi + 1) * gd]) + bsp[:, gi:gi + 1])
    gm = (u * jnp.concatenate(mixed, axis=-1)).astype(jnp.bfloat16)
    yc = _dot(gm, wc_ref[...])

    gates = jax.nn.sigmoid(g_ref[...] + bgate_ref[...])
    merged = (gates[:, :D_MODEL] * ya + gates[:, D_MODEL:2 * D_MODEL] * yb
              + gates[:, 2 * D_MODEL:] * yc)
    o_ref[...] = x_ref[...] + _dot(merged.astype(jnp.bfloat16), wout_ref[...])


def _mixer(x, q, kv, pz, guv, g, sinks, bias, wpool, pscale, vgain, wsp, bsp,
           wa, wb, wc, wout, bgate, batch):
    T = x.shape[0]
    nb = T // batch // BLK
    tail_per_blk = BLK // POOL_TAIL

    def cur(b, i):
        return (b * nb + i, 0)

    def prev(b, i):
        return (b * nb + jnp.maximum(i - 1, 0), 0)

    def prev_tail(b, i):
        return (jnp.maximum((b * nb + i) * tail_per_blk - 1, 0), 0)

    def const(shape):
        return pl.BlockSpec(shape, lambda b, i: (0,) * len(shape))

    return pl.pallas_call(
        _mixer_kernel,
        grid=(batch, nb),
        in_specs=[
            pl.BlockSpec(memory_space=pltpu.SMEM),
            pl.BlockSpec((BLK, D_MODEL), cur),
            pl.BlockSpec((BLK, Q_DIM), cur),
            pl.BlockSpec((BLK, 2 * KV_DIM), cur),
            pl.BlockSpec((BLK, 2 * KV_DIM), prev),
            pl.BlockSpec((BLK, POOL_DIM), cur),
            pl.BlockSpec((POOL_TAIL, POOL_DIM), prev_tail),
            pl.BlockSpec((BLK, 2 * GMLP_DIM), cur),
            pl.BlockSpec((BLK, GATE_DIM), cur),
            const(bias.shape), const(wpool.shape), const(pscale.shape), const(vgain.shape),
            const(wsp.shape), const(bsp.shape), const(wa.shape), const(wb.shape),
            const(wc.shape), const(wout.shape), const(bgate.shape),
        ],
        out_specs=pl.BlockSpec((BLK, D_MODEL), cur),
        out_shape=jax.ShapeDtypeStruct((T, D_MODEL), jnp.float32),
        compiler_params=pltpu.CompilerParams(
            dimension_semantics=("parallel", "parallel"), vmem_limit_bytes=VMEM_LIMIT),
        name="mixer",
    )(sinks, x, q, kv, kv, pz, pz, guv, g, bias, wpool, pscale, vgain, wsp, bsp,
      wa, wb, wc, wout, bgate)


def _topk_rows(s, k):
    rows = s.shape[0]
    rid = lax.broadcasted_iota(jnp.int32, s.shape, 0)
    vals, ids = [], []
    for _ in range(k):
        m = jnp.max(s, axis=0, keepdims=True)
        sel = jnp.min(jnp.where(s == m, rid, rows), axis=0, keepdims=True)
        vals.append(m)
        ids.append(sel)
        s = jnp.where(rid == sel, -jnp.inf, s)
    return jnp.concatenate(vals, axis=0), jnp.concatenate(ids, axis=0)


def _pick_rows(table, which, rows):
    out = jnp.zeros(which.shape, table.dtype)
    for r in range(rows):
        out = jnp.where(which == r, table[r:r + 1], out)
    return out


_CAND_WIDTHS = tuple(PEER_TOPK // (a + 1) for a in range(PEER_TOPK))
_CAND_PAD = -sum(_CAND_WIDTHS) % SUBLANES


def _route_kernel(x_ref, gain_ref, wq_ref, k1_ref, k2_ref, h_ref, expert_ref, gate_ref):
    h = _rmsnorm(x_ref[...], gain_ref[...])
    h_ref[...] = h
    qt = _dot_nt(wq_ref[...], h.astype(jnp.bfloat16)).astype(jnp.bfloat16)
    experts = []
    for hd in range(PEER_HEADS):
        base = hd * PEER_QDIM
        s1 = _dot(k1_ref[...], qt[base:base + PEER_HALF])
        s2 = _dot(k2_ref[...], qt[base + PEER_HALF:base + PEER_QDIM])
        v1, i1 = _topk_rows(s1, PEER_TOPK)
        v2, i2 = _topk_rows(s2, PEER_TOPK)
        cand = jnp.concatenate(
            [v1[a:a + 1] + v2[:w] for a, w in enumerate(_CAND_WIDTHS)]
            + [jnp.full((_CAND_PAD, BLK), -jnp.inf, jnp.float32)], axis=0)
        cand_expert = jnp.concatenate(
            [i1[a:a + 1] * N_KEYS + i2[:w] for a, w in enumerate(_CAND_WIDTHS)]
            + [jnp.zeros((_CAND_PAD, BLK), jnp.int32)], axis=0)
        top, ci = _topk_rows(cand, PEER_TOPK)
        rank = lax.broadcasted_iota(jnp.int32, (PEER_TOPK, BLK), 0)
        experts.append(_pick_rows(cand_expert, ci, sum(_CAND_WIDTHS)) * HALF_ROWS
                       + (TABLE_PAD - (rank & HALF_ROWS)))
        ex = jnp.exp(top - top[0:1])
        gate_ref[0, pl.ds(hd * PEER_TOPK, PEER_TOPK), :] = ex / jnp.sum(ex, axis=0, keepdims=True)
    expert_ref[0] = jnp.concatenate(experts, axis=0).T


def _route(x, gain, wq_t, keys1, keys2):
    T = x.shape[0]
    nblk = T // BLK
    sel_spec = pl.BlockSpec((1, PEER_SEL, BLK), lambda i: (i, 0, 0))
    tok_spec = pl.BlockSpec((1, BLK, PEER_SEL), lambda i: (i, 0, 0))
    return pl.pallas_call(
        _route_kernel,
        grid=(nblk,),
        in_specs=[
            pl.BlockSpec((BLK, D_MODEL), lambda i: (i, 0)),
            pl.BlockSpec((1, D_MODEL), lambda i: (0, 0)),
            pl.BlockSpec(wq_t.shape, lambda i: (0, 0)),
            pl.BlockSpec(keys1.shape, lambda i: (0, 0)),
            pl.BlockSpec(keys2.shape, lambda i: (0, 0)),
        ],
        out_specs=[pl.BlockSpec((BLK, D_MODEL), lambda i: (i, 0)), tok_spec, sel_spec],
        out_shape=[
            jax.ShapeDtypeStruct((T, D_MODEL), jnp.float32),
            jax.ShapeDtypeStruct((nblk, BLK, PEER_SEL), jnp.int32),
            jax.ShapeDtypeStruct((nblk, PEER_SEL, BLK), jnp.float32),
        ],
        compiler_params=pltpu.CompilerParams(
            dimension_semantics=("parallel",), vmem_limit_bytes=VMEM_LIMIT),
        name="route",
    )(x, gain, wq_t, keys1, keys2)


def _row_pair(tab_ref, wa, wb, first_half):
    words = jnp.where(first_half,
                      tab_ref[pl.ds(pl.multiple_of(wa, HALF_ROWS), SUBLANES), :],
                      tab_ref[pl.ds(pl.multiple_of(wb, HALF_ROWS), SUBLANES), :])
    lo = lax.bitcast_convert_type(lax.shift_left(words, 16), jnp.float32)
    hi = lax.bitcast_convert_type(words & jnp.int32(-65536), jnp.float32)
    return lo, hi


def _half_sums(q, sub):
    odd = (sub & 1) != 0
    c0 = jnp.where(odd, q[0] + pltpu.roll(q[0], 1, 0), q[1] + pltpu.roll(q[1], 7, 0))
    c1 = jnp.where(odd, q[2] + pltpu.roll(q[2], 1, 0), q[3] + pltpu.roll(q[3], 7, 0))
    return jnp.where((sub & 2) != 0, c0 + pltpu.roll(c0, 2, 0), c1 + pltpu.roll(c1, 6, 0))


PACK_TM = 256
TABLE_PAD = PACK_TM * HALF_ROWS


N_GROUPS = PEER_SEL // SUBLANES
V_CHUNK_SHIFT = 0
V_CHUNK_GROUPS = N_GROUPS >> V_CHUNK_SHIFT


def _by_slot(a):
    nblk, _, width = a.shape
    groups = width // SUBLANES
    return (a.reshape(nblk, BLK, groups, SUBLANES).transpose(0, 3, 1, 2)
            .reshape(nblk, SUBLANES, BLK * groups))


def _slot_copies(src_hbm, dst_slots, sems, blk, half):
    n = src_hbm.shape[2]
    return [pltpu.make_async_copy(src_hbm.at[blk, j], dst_slots[j].at[pl.ds(half * n, n)],
                                  sems.at[half, j])
            for j in range(SUBLANES)]


def _fetch_slots(sources):
    i = pl.program_id(0)
    half = i % 2

    @pl.when(i == 0)
    def _():
        for src, dst, sems in sources:
            for copy in _slot_copies(src, dst, sems, 0, 0):
                copy.start()

    for src, dst, sems in sources:
        for copy in _slot_copies(src, dst, sems, i, half):
            copy.wait()

    @pl.when(i + 1 < pl.num_programs(0))
    def _():
        for src, dst, sems in sources:
            for copy in _slot_copies(src, dst, sems, i + 1, 1 - half):
                copy.start()

    return half


def _bf16_bits(x):
    b = lax.bitcast_convert_type(x, jnp.int32)
    return (b + 0x7FFF + ((b >> 16) & 1)) & jnp.int32(-65536)


def _peer_u_kernel(expert_hbm, h_ref, gate_ref, tab_ref, coeff_ref, *scratch):
    expert_s = scratch[:SUBLANES]
    sems, tiles_ref, act_ref = scratch[SUBLANES:]
    base = _fetch_slots([(expert_hbm, expert_s, sems)]) * (BLK * N_GROUPS)
    sub = lax.broadcasted_iota(jnp.int32, (SUBLANES, LANES), 0)
    first_half = sub < HALF_ROWS
    lane = lax.broadcasted_iota(jnp.int32, (PEER_SEL, BLK), 1)
    tiles_ref[...] = jnp.zeros(tiles_ref.shape, jnp.float32)
    act_ref[...] = jnp.zeros(act_ref.shape, jnp.float32)

    def reduce_token(t):
        src = pl.ds(pl.multiple_of((t & 1) * PEER_SEL, PEER_SEL), PEER_SEL)
        s = jnp.sum(tiles_ref[src, :], axis=1, keepdims=True)
        act_ref[...] = jnp.where(lane == t, s, act_ref[...])

    def token(t, carry):
        reduce_token(t - 1)
        ht = h_ref[t]
        h_lo = jnp.concatenate([ht[:HALF_ROWS], ht[:HALF_ROWS]], axis=0)
        h_hi = jnp.concatenate([ht[HALF_ROWS:], ht[HALF_ROWS:]], axis=0)
        for grp in range(N_GROUPS):
            r = base + t * N_GROUPS + grp
            q = []
            for i in reversed(range(HALF_ROWS)):
                lo, hi = _row_pair(tab_ref, expert_s[i][r], expert_s[i + HALF_ROWS][r], first_half)
                q.append(lo * h_lo + hi * h_hi)
            dst = (t & 1) * PEER_SEL + grp * SUBLANES
            tiles_ref[pl.ds(pl.multiple_of(dst, SUBLANES), SUBLANES), :] = _half_sums(q, sub)
        return carry

    lax.fori_loop(0, BLK, token, 0)
    reduce_token(BLK - 1)
    coeff = gate_ref[0] * _gelu(act_ref[...])
    partner = pltpu.roll(coeff, PEER_SEL - HALF_ROWS, 0)
    coeff_ref[0] = (_bf16_bits(partner) | lax.shift_right_logical(_bf16_bits(coeff), 16)).T


def _peer_v_kernel(expert_hbm, coeff_hbm, x_ref, tab_ref, o_ref, *scratch):
    expert_s, coeff_s = scratch[:SUBLANES], scratch[SUBLANES:2 * SUBLANES]
    expert_sems, coeff_sems = scratch[2 * SUBLANES:]
    base = _fetch_slots([(expert_hbm, expert_s, expert_sems),
                         (coeff_hbm, coeff_s, coeff_sems)]) * (BLK * N_GROUPS)
    first_half = lax.broadcasted_iota(jnp.int32, (SUBLANES, LANES), 0) < HALF_ROWS
    coeff_shift = jnp.where(first_half, 16, 0)
    o_ref[...] = x_ref[...]

    def chunk(n, carry):
        acc_lo = [jnp.zeros((SUBLANES, LANES), jnp.float32) for _ in range(2)]
        acc_hi = [jnp.zeros((SUBLANES, LANES), jnp.float32) for _ in range(2)]
        for grp in range(V_CHUNK_GROUPS):
            r = base + n * V_CHUNK_GROUPS + grp
            for i in range(HALF_ROWS):
                lo, hi = _row_pair(tab_ref, expert_s[i][r], expert_s[i + HALF_ROWS][r], first_half)
                word = jnp.full((SUBLANES, LANES), coeff_s[i][r], jnp.int32)
                c = lax.bitcast_convert_type(
                    lax.shift_left(word, coeff_shift) & jnp.int32(-65536), jnp.float32)
                acc_lo[i % 2] = acc_lo[i % 2] + c * lo
                acc_hi[i % 2] = acc_hi[i % 2] + c * hi
        y_lo = acc_lo[0] + acc_lo[1]
        y_hi = acc_hi[0] + acc_hi[1]
        y = jnp.concatenate([y_lo[:HALF_ROWS] + y_lo[HALF_ROWS:],
                             y_hi[:HALF_ROWS] + y_hi[HALF_ROWS:]], axis=0)
        t = n >> V_CHUNK_SHIFT
        o_ref[t] = o_ref[t] + y
        return carry

    lax.fori_loop(0, BLK << V_CHUNK_SHIFT, chunk, 0)


def _table_spec(tab):
    return pl.BlockSpec(tab.shape, lambda i: (0, 0), pipeline_mode=pl.Buffered(1))


def _slot_scratch(groups):
    return [pltpu.SMEM((2 * BLK * groups,), jnp.int32) for _ in range(SUBLANES)]


def _slot_sems():
    return pltpu.SemaphoreType.DMA((2, SUBLANES))


def _peer_u(expert, h3, gate, tab):
    nblk = expert.shape[0]
    return pl.pallas_call(
        _peer_u_kernel,
        grid=(nblk,),
        in_specs=[
            pl.BlockSpec(memory_space=pl.ANY),
            pl.BlockSpec((BLK, SUBLANES, LANES), lambda i: (i, 0, 0)),
            pl.BlockSpec((1, PEER_SEL, BLK), lambda i: (i, 0, 0)),
            _table_spec(tab),
        ],
        out_specs=pl.BlockSpec((1, BLK, PEER_SEL), lambda i: (i, 0, 0)),
        out_shape=jax.ShapeDtypeStruct((nblk, BLK, PEER_SEL), jnp.int32),
        scratch_shapes=_slot_scratch(N_GROUPS) + [
            _slot_sems(),
            pltpu.VMEM((2 * PEER_SEL, LANES), jnp.float32),
            pltpu.VMEM((PEER_SEL, BLK), jnp.float32),
        ],
        compiler_params=pltpu.CompilerParams(
            dimension_semantics=("arbitrary",), vmem_limit_bytes=VMEM_LIMIT),
        name="peer_u",
    )(expert, h3, gate, tab)


def _peer_v(expert, coeff, x3, tab):
    nblk = expert.shape[0]
    tok_spec = pl.BlockSpec((BLK, SUBLANES, LANES), lambda i: (i, 0, 0))
    return pl.pallas_call(
        _peer_v_kernel,
        grid=(nblk,),
        in_specs=[
            pl.BlockSpec(memory_space=pl.ANY),
            pl.BlockSpec(memory_space=pl.ANY),
            tok_spec,
            _table_spec(tab),
        ],
        out_specs=tok_spec,
        out_shape=jax.ShapeDtypeStruct(x3.shape, jnp.float32),
        scratch_shapes=(_slot_scratch(N_GROUPS) + _slot_scratch(N_GROUPS)
                        + [_slot_sems(), _slot_sems()]),
        compiler_params=pltpu.CompilerParams(
            dimension_semantics=("arbitrary",), vmem_limit_bytes=VMEM_LIMIT),
        name="peer_v",
    )(expert, coeff, x3, tab)


NORM_TM = 512


def _final_norm_kernel(x_ref, gain_ref, o_ref):
    o_ref[...] = _rmsnorm(x_ref[...], gain_ref[...])


def _final_norm(x, gain):
    T = x.shape[0]
    return pl.pallas_call(
        _final_norm_kernel,
        grid=(T // NORM_TM,),
        in_specs=[pl.BlockSpec((NORM_TM, D_MODEL), lambda i: (i, 0)),
                  pl.BlockSpec((1, D_MODEL), lambda i: (0, 0))],
        out_specs=pl.BlockSpec((NORM_TM, D_MODEL), lambda i: (i, 0)),
        out_shape=jax.ShapeDtypeStruct((T, D_MODEL), jnp.float32),
        compiler_params=pltpu.CompilerParams(dimension_semantics=("parallel",)),
        name="final_norm",
    )(x, gain)


def _t5_causal_bucket(dist):
    n = jnp.maximum(dist, 0)
    max_exact = N_BUCKETS // 2
    nf = jnp.maximum(n, 1).astype(jnp.float32)
    large = max_exact + (jnp.log(nf / max_exact) / math.log(MAX_DISTANCE / max_exact)
                         * (N_BUCKETS - max_exact)).astype(jnp.int32)
    large = jnp.minimum(large, N_BUCKETS - 1)
    return jnp.where(n < max_exact, n, large)


def _attention_bias(rel_bias):
    qi = jnp.arange(BLK)[:, None]
    kj = jnp.arange(2 * BLK)[None, :]
    bucket = _t5_causal_bucket(qi + BLK - kj)
    onehot = (bucket[:, :, None] == jnp.arange(N_BUCKETS)).astype(jnp.float32)
    return jnp.einsum('qkb,bh->hqk', onehot, rel_bias.astype(jnp.float32),
                      precision=lax.Precision.HIGHEST)


def _pack_table(tabs, layer):
    steps = N_EXPERTS // PACK_TM + 2
    return pl.pallas_call(
        _pack_kernel,
        grid=(steps,),
        in_specs=[pl.BlockSpec((1, PACK_TM, D_MODEL),
                               lambda i: (layer, jnp.clip(i - 1, 0, N_EXPERTS // PACK_TM - 1), 0))],
        out_specs=pl.BlockSpec((TABLE_PAD, LANES), lambda i: (i, 0)),
        out_shape=jax.ShapeDtypeStruct((steps * TABLE_PAD, LANES), jnp.int32),
        compiler_params=pltpu.CompilerParams(dimension_semantics=("parallel",)),
        name="pack_table",
    )(tabs)


def _pack_kernel(tab_ref, out_ref):
    i = pl.program_id(0)
    interior = (i > 0) & (i < pl.num_programs(0) - 1)

    @pl.when(interior)
    def _():
        bits = _bf16_bits(tab_ref[0])
        words = bits[:, D_MODEL // 2:] | lax.shift_right_logical(bits[:, :D_MODEL // 2], 16)
        for s in range(HALF_ROWS):
            out_ref[pl.ds(s, PACK_TM, stride=HALF_ROWS), :] = words[:, s * LANES:(s + 1) * LANES]

    @pl.when(jnp.logical_not(interior))
    def _():
        out_ref[...] = jnp.zeros(out_ref.shape, jnp.int32)


def kernel(x, rel_bias, final_gain, attn_norm_gain, w_in, b_gate, attn_sinks, w_pool, pool_scale, gmlp_v_gain, w_spatial, b_spatial, w_branch_a, w_branch_b, w_branch_c, w_out, ffn_norm_gain, peer_w_query, peer_sub_keys1, peer_sub_keys2, peer_expert_u, peer_expert_v):
    B, S, D = x.shape
    T = B * S
    depth = w_in.shape[0]
    bf16 = jnp.bfloat16
    bias = _attention_bias(rel_bias)
    xt = x.reshape(T, D)
    for l in range(depth):
        q, kv, pz, guv, g = _in_proj(xt, attn_norm_gain[l][None], w_in[l].astype(bf16))
        xt = _mixer(xt, q, kv, pz, guv, g, attn_sinks[l], bias,
                    w_pool[l].astype(bf16), pool_scale[l][None], gmlp_v_gain[l][None],
                    w_spatial[l], b_spatial[l].T,
                    w_branch_a[l].astype(bf16), w_branch_b[l].astype(bf16),
                    w_branch_c[l].astype(bf16), w_out[l].astype(bf16), b_gate[l][None], B)
        h, expert, gate = _route(xt, ffn_norm_gain[l][None], peer_w_query[l].T.astype(bf16),
                                 peer_sub_keys1[l].astype(bf16), peer_sub_keys2[l].astype(bf16))
        expert = _by_slot(expert)
        coeff = _peer_u(expert, h.reshape(T, SUBLANES, LANES), gate, _pack_table(peer_expert_u, l))
        xt = _peer_v(expert, _by_slot(coeff), xt.reshape(T, SUBLANES, LANES),
                     _pack_table(peer_expert_v, l)).reshape(T, D)
    return _final_norm(xt, final_gain[None]).reshape(B, S, D)
```

```python
import functools
import math

import jax
import jax.numpy as jnp
from jax import lax
from jax.experimental import pallas as pl
from jax.experimental.pallas import tpu as pltpu

D_MODEL = 1024
N_HEADS = 8
N_KV_HEADS = 2
HEAD_DIM = 64
GQA_GROUP = N_HEADS // N_KV_HEADS
WINDOW = 128
BLK = 128
Q_DIM = N_HEADS * HEAD_DIM
KV_DIM = N_KV_HEADS * HEAD_DIM
N_BUCKETS = 32
MAX_DISTANCE = 128
POOL_WINDOWS = (2, 4, 8, 16)
POOL_GROUP_DIM = 128
POOL_DIM = 512
POOL_TAIL = 16
GMLP_DIM = 512
GMLP_GROUPS = 4
GATE_DIM = 3 * D_MODEL
PEER_HEADS = 8
N_KEYS = 128
PEER_TOPK = 16
PEER_QDIM = 256
PEER_HALF = 128
PEER_SEL = PEER_HEADS * PEER_TOPK
EPS = 1e-6
NEG_INF = -1e30

SUBLANES = 8
LANES = 128
N_EXPERTS = N_KEYS * N_KEYS
HALF_ROWS = SUBLANES // 2
VMEM_LIMIT = 56 * 1024 * 1024


def _rmsnorm(x, gain):
    return x * lax.rsqrt(jnp.mean(x * x, axis=-1, keepdims=True) + EPS) * gain


def _gelu(x):
    c = math.sqrt(2.0 / math.pi)
    return x * (0.5 * (1.0 + jnp.tanh(c * (x + 0.044715 * (x * x * x)))))


def _dot(a, b):
    return jnp.dot(a, b, preferred_element_type=jnp.float32)


def _dot_nt(a, b):
    return lax.dot_general(a, b, (((1,), (1,)), ((), ())), preferred_element_type=jnp.float32)


IN_TM = 256


def _in_proj_kernel(x_ref, gain_ref, w_ref, q_ref, kv_ref, pz_ref, guv_ref, g_ref):
    h = _rmsnorm(x_ref[...], gain_ref[...]).astype(jnp.bfloat16)
    o = 0
    for ref in (q_ref, kv_ref, pz_ref, guv_ref, g_ref):
        n = ref.shape[1]
        ref[...] = _dot(h, w_ref[:, o:o + n]).astype(ref.dtype)
        o += n


def _in_proj(x, gain, w_in):
    T = x.shape[0]
    widths = (Q_DIM, 2 * KV_DIM, POOL_DIM, 2 * GMLP_DIM, GATE_DIM)
    dtypes = (jnp.bfloat16, jnp.bfloat16, jnp.float32, jnp.float32, jnp.float32)
    return pl.pallas_call(
        _in_proj_kernel,
        grid=(T // IN_TM,),
        in_specs=[
            pl.BlockSpec((IN_TM, D_MODEL), lambda i: (i, 0)),
            pl.BlockSpec((1, D_MODEL), lambda i: (0, 0)),
            pl.BlockSpec(w_in.shape, lambda i: (0, 0)),
        ],
        out_specs=[pl.BlockSpec((IN_TM, n), lambda i: (i, 0)) for n in widths],
        out_shape=[jax.ShapeDtypeStruct((T, n), d) for n, d in zip(widths, dtypes)],
        compiler_params=pltpu.CompilerParams(
            dimension_semantics=("parallel",), vmem_limit_bytes=VMEM_LIMIT),
        name="in_proj",
    )(x, gain, w_in)


def _mixer_kernel(sinks_ref, x_ref, q_ref, kvc_ref, kvp_ref, pzc_ref, pzp_ref, guv_ref, g_ref,
                  bias_ref, wpool_ref, pscale_ref, vgain_ref, wsp_ref, bsp_ref,
                  wa_ref, wb_ref, wc_ref, wout_ref, bgate_ref, o_ref):
    i = pl.program_id(1)
    first = i == 0

    kvc = kvc_ref[...]
    kvp = kvp_ref[...]
    kk = jnp.concatenate([kvp[:, :KV_DIM], kvc[:, :KV_DIM]], axis=0)
    vv = jnp.concatenate([kvp[:, KV_DIM:], kvc[:, KV_DIM:]], axis=0)
    qi = lax.broadcasted_iota(jnp.int32, (BLK, 2 * BLK), 0)
    kj = lax.broadcasted_iota(jnp.int32, (BLK, 2 * BLK), 1)
    dist = qi + BLK - kj
    visible = (dist >= 0) & (dist < WINDOW) & jnp.logical_not(first & (kj < BLK))
    q = q_ref[...]
    heads = []
    for h in range(N_HEADS):
        g = h // GQA_GROUP
        qh = q[:, h * HEAD_DIM:(h + 1) * HEAD_DIM]
        kh = kk[:, g * HEAD_DIM:(g + 1) * HEAD_DIM]
        vh = vv[:, g * HEAD_DIM:(g + 1) * HEAD_DIM]
        s = _dot_nt(qh, kh) * (HEAD_DIM ** -0.5) + bias_ref[h]
        s = jnp.where(visible, s, NEG_INF)
        sink = sinks_ref[h]
        m = jnp.maximum(jnp.max(s, axis=-1, keepdims=True), sink)
        p = jnp.exp(s - m)
        denom = jnp.sum(p, axis=-1, keepdims=True) + jnp.exp(sink - m)
        heads.append(_dot(p.astype(jnp.bfloat16), vh) / denom)
    attn = jnp.concatenate(heads, axis=-1).astype(jnp.bfloat16)
    ya = _dot(attn, wa_ref[...])

    zc = pzc_ref[...]
    zp = jnp.where(first, 0.0, pzp_ref[...])
    ext = jnp.concatenate([zp, zc], axis=0)
    pos = i * BLK + lax.broadcasted_iota(jnp.int32, (BLK, 1), 0)
    pooled = []
    for gi, w in enumerate(POOL_WINDOWS):
        lo = gi * POOL_GROUP_DIM
        s = ext[:, lo:lo + POOL_GROUP_DIM]
        width = 1
        while width < w:
            s = s + pltpu.roll(s, width, 0)
            width *= 2
        count = jnp.minimum(pos + 1, w).astype(jnp.float32)
        d = s[POOL_TAIL:] / count - zc[:, lo:lo + POOL_GROUP_DIM]
        pooled.append(_dot(d.astype(jnp.bfloat16), wpool_ref[gi]))
    pool = (jnp.concatenate(pooled, axis=-1) * pscale_ref[...]).astype(jnp.bfloat16)
    yb = _dot(pool, wb_ref[...])

    guv = guv_ref[...]
    u = _gelu(guv[:, :GMLP_DIM])
    v = _rmsnorm(_gelu(guv[:, GMLP_DIM:]), vgain_ref[...]).astype(jnp.bfloat16)
    ti = lax.broadcasted_iota(jnp.int32, (BLK, BLK), 0)
    si = lax.broadcasted_iota(jnp.int32, (BLK, BLK), 1)
    bsp = bsp_ref[...]
    mixed = []
    gd = GMLP_DIM // GMLP_GROUPS
    for gi in range(GMLP_GROUPS):
        w = jnp.where(si <= ti, wsp_ref[gi], 0.0).astype(jnp.bfloat16)
        mixed.append(_dot(w, v[:, gi * gd:(gi + 1) * gd]) + bsp[:, gi:gi + 1])
    gm = (u * jnp.concatenate(mixed, axis=-1)).astype(jnp.bfloat16)
    yc = _dot(gm, wc_ref[...])

    gates = jax.nn.sigmoid(g_ref[...] + bgate_ref[...])
    merged = (gates[:, :D_MODEL] * ya + gates[:, D_MODEL:2 * D_MODEL] * yb
              + gates[:, 2 * D_MODEL:] * yc)
    o_ref[...] = x_ref[...] + _dot(merged.astype(jnp.bfloat16), wout_ref[...])


def _mixer(x, q, kv, pz, guv, g, sinks, bias, wpool, pscale, vgain, wsp, bsp,
           wa, wb, wc, wout, bgate, batch):
    T = x.shape[0]
    nb = T // batch // BLK
    tail_per_blk = BLK // POOL_TAIL

    def cur(b, i):
        return (b * nb + i, 0)

    def prev(b, i):
        return (b * nb + jnp.maximum(i - 1, 0), 0)

    def prev_tail(b, i):
        return (jnp.maximum((b * nb + i) * tail_per_blk - 1, 0), 0)

    def const(shape):
        return pl.BlockSpec(shape, lambda b, i: (0,) * len(shape))

    return pl.pallas_call(
        _mixer_kernel,
        grid=(batch, nb),
        in_specs=[
            pl.BlockSpec(memory_space=pltpu.SMEM),
            pl.BlockSpec((BLK, D_MODEL), cur),
            pl.BlockSpec((BLK, Q_DIM), cur),
            pl.BlockSpec((BLK, 2 * KV_DIM), cur),
            pl.BlockSpec((BLK, 2 * KV_DIM), prev),
            pl.BlockSpec((BLK, POOL_DIM), cur),
            pl.BlockSpec((POOL_TAIL, POOL_DIM), prev_tail),
            pl.BlockSpec((BLK, 2 * GMLP_DIM), cur),
            pl.BlockSpec((BLK, GATE_DIM), cur),
            const(bias.shape), const(wpool.shape), const(pscale.shape), const(vgain.shape),
            const(wsp.shape), const(bsp.shape), const(wa.shape), const(wb.shape),
            const(wc.shape), const(wout.shape), const(bgate.shape),
        ],
        out_specs=pl.BlockSpec((BLK, D_MODEL), cur),
        out_shape=jax.ShapeDtypeStruct((T, D_MODEL), jnp.float32),
        compiler_params=pltpu.CompilerParams(
            dimension_semantics=("parallel", "parallel"), vmem_limit_bytes=VMEM_LIMIT),
        name="mixer",
    )(sinks, x, q, kv, kv, pz, pz, guv, g, bias, wpool, pscale, vgain, wsp, bsp,
      wa, wb, wc, wout, bgate)


def _topk_rows(s, k):
    rows = s.shape[0]
    rid = lax.broadcasted_iota(jnp.int32, s.shape, 0)
    vals, ids = [], []
    for _ in range(k):
        m = jnp.max(s, axis=0, keepdims=True)
        sel = jnp.min(jnp.where(s == m, rid, rows), axis=0, keepdims=True)
        vals.append(m)
        ids.append(sel)
        s = jnp.where(rid == sel, -jnp.inf, s)
    return jnp.concatenate(vals, axis=0), jnp.concatenate(ids, axis=0)


def _pick_rows(table, which, rows):
    out = jnp.zeros(which.shape, table.dtype)
    for r in range(rows):
        out = jnp.where(which == r, table[r:r + 1], out)
    return out


_CAND_WIDTHS = tuple(PEER_TOPK // (a + 1) for a in range(PEER_TOPK))
_CAND_PAD = -sum(_CAND_WIDTHS) % SUBLANES


def _route_kernel(x_ref, gain_ref, wq_ref, k1_ref, k2_ref, h_ref, expert_ref, gate_ref):
    h = _rmsnorm(x_ref[...], gain_ref[...])
    h_ref[...] = h.reshape(BLK, SUBLANES, LANES)
    qt = _dot_nt(wq_ref[...], h.astype(jnp.bfloat16)).astype(jnp.bfloat16)
    experts = []
    for hd in range(PEER_HEADS):
        base = hd * PEER_QDIM
        s1 = _dot(k1_ref[...], qt[base:base + PEER_HALF])
        s2 = _dot(k2_ref[...], qt[base + PEER_HALF:base + PEER_QDIM])
        v1, i1 = _topk_rows(s1, PEER_TOPK)
        v2, i2 = _topk_rows(s2, PEER_TOPK)
        cand = jnp.concatenate(
            [v1[a:a + 1] + v2[:w] for a, w in enumerate(_CAND_WIDTHS)]
            + [jnp.full((_CAND_PAD, BLK), -jnp.inf, jnp.float32)], axis=0)
        cand_expert = jnp.concatenate(
            [i1[a:a + 1] * N_KEYS + i2[:w] for a, w in enumerate(_CAND_WIDTHS)]
            + [jnp.zeros((_CAND_PAD, BLK), jnp.int32)], axis=0)
        top, ci = _topk_rows(cand, PEER_TOPK)
        rank = lax.broadcasted_iota(jnp.int32, (PEER_TOPK, BLK), 0)
        experts.append(_pick_rows(cand_expert, ci, sum(_CAND_WIDTHS)) * HALF_ROWS
                       + (TABLE_PAD - (rank & HALF_ROWS)))
        ex = jnp.exp(top - top[0:1])
        gate_ref[0, pl.ds(hd * PEER_TOPK, PEER_TOPK), :] = ex / jnp.sum(ex, axis=0, keepdims=True)
    expert_ref[0] = jnp.concatenate(experts, axis=0).T


def _route(x, gain, wq_t, keys1, keys2):
    T = x.shape[0]
    nblk = T // BLK
    sel_spec = pl.BlockSpec((1, PEER_SEL, BLK), lambda i: (i, 0, 0))
    tok_spec = pl.BlockSpec((1, BLK, PEER_SEL), lambda i: (i, 0, 0))
    return pl.pallas_call(
        _route_kernel,
        grid=(nblk,),
        in_specs=[
            pl.BlockSpec((BLK, D_MODEL), lambda i: (i, 0)),
            pl.BlockSpec((1, D_MODEL), lambda i: (0, 0)),
            pl.BlockSpec(wq_t.shape, lambda i: (0, 0)),
            pl.BlockSpec(keys1.shape, lambda i: (0, 0)),
            pl.BlockSpec(keys2.shape, lambda i: (0, 0)),
        ],
        out_specs=[pl.BlockSpec((BLK, SUBLANES, LANES), lambda i: (i, 0, 0)), tok_spec, sel_spec],
        out_shape=[
            jax.ShapeDtypeStruct((T, SUBLANES, LANES), jnp.float32),
            jax.ShapeDtypeStruct((nblk, BLK, PEER_SEL), jnp.int32),
            jax.ShapeDtypeStruct((nblk, PEER_SEL, BLK), jnp.float32),
        ],
        compiler_params=pltpu.CompilerParams(
            dimension_semantics=("parallel",), vmem_limit_bytes=VMEM_LIMIT),
        name="route",
    )(x, gain, wq_t, keys1, keys2)


def _row_pair(tab_ref, wa, wb, first_half):
    words = jnp.where(first_half,
                      tab_ref[pl.ds(pl.multiple_of(wa, HALF_ROWS), SUBLANES), :],
                      tab_ref[pl.ds(pl.multiple_of(wb, HALF_ROWS), SUBLANES), :])
    lo = lax.bitcast_convert_type(lax.shift_left(words, 16), jnp.float32)
    hi = lax.bitcast_convert_type(words & jnp.int32(-65536), jnp.float32)
    return lo, hi


def _half_sums(q, sub):
    odd = (sub & 1) != 0
    c0 = jnp.where(odd, q[0] + pltpu.roll(q[0], 1, 0), q[1] + pltpu.roll(q[1], 7, 0))
    c1 = jnp.where(odd, q[2] + pltpu.roll(q[2], 1, 0), q[3] + pltpu.roll(q[3], 7, 0))
    return jnp.where((sub & 2) != 0, c0 + pltpu.roll(c0, 2, 0), c1 + pltpu.roll(c1, 6, 0))


PACK_TM = 256
TABLE_PAD = PACK_TM * HALF_ROWS


N_GROUPS = PEER_SEL // SUBLANES
V_CHUNK_SHIFT = 0
V_CHUNK_GROUPS = N_GROUPS >> V_CHUNK_SHIFT


def _by_slot(a):
    nblk, _, width = a.shape
    groups = width // SUBLANES
    return (a.reshape(nblk, BLK, groups, SUBLANES).transpose(0, 3, 1, 2)
            .reshape(nblk, SUBLANES, BLK * groups))


def _slot_copies(src_hbm, dst_slots, sems, blk, half):
    n = src_hbm.shape[2]
    return [pltpu.make_async_copy(src_hbm.at[blk, j], dst_slots[j].at[pl.ds(half * n, n)],
                                  sems.at[half, j])
            for j in range(SUBLANES)]


def _fetch_slots(sources):
    i = pl.program_id(0)
    half = i % 2

    @pl.when(i == 0)
    def _():
        for src, dst, sems in sources:
            for copy in _slot_copies(src, dst, sems, 0, 0):
                copy.start()

    for src, dst, sems in sources:
        for copy in _slot_copies(src, dst, sems, i, half):
            copy.wait()

    @pl.when(i + 1 < pl.num_programs(0))
    def _():
        for src, dst, sems in sources:
            for copy in _slot_copies(src, dst, sems, i + 1, 1 - half):
                copy.start()

    return half


def _bf16_bits(x):
    b = lax.bitcast_convert_type(x, jnp.int32)
    return (b + 0x7FFF + ((b >> 16) & 1)) & jnp.int32(-65536)


def _peer_u_kernel(expert_hbm, h_ref, gate_ref, tab_ref, coeff_ref, *scratch):
    expert_s = scratch[:SUBLANES]
    sems, tiles_ref, act_ref = scratch[SUBLANES:]
    base = _fetch_slots([(expert_hbm, expert_s, sems)]) * (BLK * N_GROUPS)
    sub = lax.broadcasted_iota(jnp.int32, (SUBLANES, LANES), 0)
    first_half = sub < HALF_ROWS
    lane = lax.broadcasted_iota(jnp.int32, (PEER_SEL, BLK), 1)
    tiles_ref[...] = jnp.zeros(tiles_ref.shape, jnp.float32)
    act_ref[...] = jnp.zeros(act_ref.shape, jnp.float32)

    def reduce_token(t):
        src = pl.ds(pl.multiple_of((t & 1) * PEER_SEL, PEER_SEL), PEER_SEL)
        s = jnp.sum(tiles_ref[src, :], axis=1, keepdims=True)
        act_ref[...] = jnp.where(lane == t, s, act_ref[...])

    def token(t, carry):
        reduce_token(t - 1)
        ht = h_ref[t]
        h_lo = jnp.concatenate([ht[:HALF_ROWS], ht[:HALF_ROWS]], axis=0)
        h_hi = jnp.concatenate([ht[HALF_ROWS:], ht[HALF_ROWS:]], axis=0)
        for grp in range(N_GROUPS):
            r = base + t * N_GROUPS + grp
            q = []
            for i in reversed(range(HALF_ROWS)):
                lo, hi = _row_pair(tab_ref, expert_s[i][r], expert_s[i + HALF_ROWS][r], first_half)
                q.append(lo * h_lo + hi * h_hi)
            dst = (t & 1) * PEER_SEL + grp * SUBLANES
            tiles_ref[pl.ds(pl.multiple_of(dst, SUBLANES), SUBLANES), :] = _half_sums(q, sub)
        return carry

    lax.fori_loop(0, BLK, token, 0)
    reduce_token(BLK - 1)
    coeff = gate_ref[0] * _gelu(act_ref[...])
    partner = pltpu.roll(coeff, PEER_SEL - HALF_ROWS, 0)
    coeff_ref[0] = (_bf16_bits(partner) | lax.shift_right_logical(_bf16_bits(coeff), 16)).T


def _peer_v_kernel(expert_hbm, coeff_hbm, x_ref, tab_ref, o_ref, *scratch):
    expert_s, coeff_s = scratch[:SUBLANES], scratch[SUBLANES:2 * SUBLANES]
    expert_sems, coeff_sems = scratch[2 * SUBLANES:]
    base = _fetch_slots([(expert_hbm, expert_s, expert_sems),
                         (coeff_hbm, coeff_s, coeff_sems)]) * (BLK * N_GROUPS)
    first_half = lax.broadcasted_iota(jnp.int32, (SUBLANES, LANES), 0) < HALF_ROWS
    coeff_shift = jnp.where(first_half, 16, 0)
    o_ref[...] = x_ref[...]

    def chunk(n, carry):
        acc_lo = [jnp.zeros((SUBLANES, LANES), jnp.float32) for _ in range(2)]
        acc_hi = [jnp.zeros((SUBLANES, LANES), jnp.float32) for _ in range(2)]
        for grp in range(V_CHUNK_GROUPS):
            r = base + n * V_CHUNK_GROUPS + grp
            for i in range(HALF_ROWS):
                lo, hi = _row_pair(tab_ref, expert_s[i][r], expert_s[i + HALF_ROWS][r], first_half)
                word = jnp.full((SUBLANES, LANES), coeff_s[i][r], jnp.int32)
                c = lax.bitcast_convert_type(
                    lax.shift_left(word, coeff_shift) & jnp.int32(-65536), jnp.float32)
                acc_lo[i % 2] = acc_lo[i % 2] + c * lo
                acc_hi[i % 2] = acc_hi[i % 2] + c * hi
        y_lo = acc_lo[0] + acc_lo[1]
        y_hi = acc_hi[0] + acc_hi[1]
        y = jnp.concatenate([y_lo[:HALF_ROWS] + y_lo[HALF_ROWS:],
                             y_hi[:HALF_ROWS] + y_hi[HALF_ROWS:]], axis=0)
        t = n >> V_CHUNK_SHIFT
        o_ref[t] = o_ref[t] + y
        return carry

    lax.fori_loop(0, BLK << V_CHUNK_SHIFT, chunk, 0)


def _table_spec(tab):
    return pl.BlockSpec(tab.shape, lambda i: (0, 0), pipeline_mode=pl.Buffered(1))


def _slot_scratch(groups):
    return [pltpu.SMEM((2 * BLK * groups,), jnp.int32) for _ in range(SUBLANES)]


def _slot_sems():
    return pltpu.SemaphoreType.DMA((2, SUBLANES))


def _peer_u(expert, h3, gate, tab):
    nblk = expert.shape[0]
    return pl.pallas_call(
        _peer_u_kernel,
        grid=(nblk,),
        in_specs=[
            pl.BlockSpec(memory_space=pl.ANY),
            pl.BlockSpec((BLK, SUBLANES, LANES), lambda i: (i, 0, 0)),
            pl.BlockSpec((1, PEER_SEL, BLK), lambda i: (i, 0, 0)),
            _table_spec(tab),
        ],
        out_specs=pl.BlockSpec((1, BLK, PEER_SEL), lambda i: (i, 0, 0)),
        out_shape=jax.ShapeDtypeStruct((nblk, BLK, PEER_SEL), jnp.int32),
        scratch_shapes=_slot_scratch(N_GROUPS) + [
            _slot_sems(),
            pltpu.VMEM((2 * PEER_SEL, LANES), jnp.float32),
            pltpu.VMEM((PEER_SEL, BLK), jnp.float32),
        ],
        compiler_params=pltpu.CompilerParams(
            dimension_semantics=("arbitrary",), vmem_limit_bytes=VMEM_LIMIT),
        name="peer_u",
    )(expert, h3, gate, tab)


def _peer_v(expert, coeff, x3, tab):
    nblk = expert.shape[0]
    tok_spec = pl.BlockSpec((BLK, SUBLANES, LANES), lambda i: (i, 0, 0))
    return pl.pallas_call(
        _peer_v_kernel,
        grid=(nblk,),
        in_specs=[
            pl.BlockSpec(memory_space=pl.ANY),
            pl.BlockSpec(memory_space=pl.ANY),
            tok_spec,
            _table_spec(tab),
        ],
        out_specs=tok_spec,
        out_shape=jax.ShapeDtypeStruct(x3.shape, jnp.float32),
        scratch_shapes=(_slot_scratch(N_GROUPS) + _slot_scratch(N_GROUPS)
                        + [_slot_sems(), _slot_sems()]),
        compiler_params=pltpu.CompilerParams(
            dimension_semantics=("arbitrary",), vmem_limit_bytes=VMEM_LIMIT),
        name="peer_v",
    )(expert, coeff, x3, tab)


NORM_TM = 512


def _final_norm_kernel(x_ref, gain_ref, o_ref):
    o_ref[...] = _rmsnorm(x_ref[...], gain_ref[...])


def _final_norm(x, gain):
    T = x.shape[0]
    return pl.pallas_call(
        _final_norm_kernel,
        grid=(T // NORM_TM,),
        in_specs=[pl.BlockSpec((NORM_TM, D_MODEL), lambda i: (i, 0)),
                  pl.BlockSpec((1, D_MODEL), lambda i: (0, 0))],
        out_specs=pl.BlockSpec((NORM_TM, D_MODEL), lambda i: (i, 0)),
        out_shape=jax.ShapeDtypeStruct((T, D_MODEL), jnp.float32),
        compiler_params=pltpu.CompilerParams(dimension_semantics=("parallel",)),
        name="final_norm",
    )(x, gain)


def _t5_causal_bucket(dist):
    n = jnp.maximum(dist, 0)
    max_exact = N_BUCKETS // 2
    nf = jnp.maximum(n, 1).astype(jnp.float32)
    large = max_exact + (jnp.log(nf / max_exact) / math.log(MAX_DISTANCE / max_exact)
                         * (N_BUCKETS - max_exact)).astype(jnp.int32)
    large = jnp.minimum(large, N_BUCKETS - 1)
    return jnp.where(n < max_exact, n, large)


def _attention_bias(rel_bias):
    qi = jnp.arange(BLK)[:, None]
    kj = jnp.arange(2 * BLK)[None, :]
    bucket = _t5_causal_bucket(qi + BLK - kj)
    onehot = (bucket[:, :, None] == jnp.arange(N_BUCKETS)).astype(jnp.float32)
    return jnp.einsum('qkb,bh->hqk', onehot, rel_bias.astype(jnp.float32),
                      precision=lax.Precision.HIGHEST)


def _pack_table(tabs, layer):
    steps = N_EXPERTS // PACK_TM + 2
    return pl.pallas_call(
        _pack_kernel,
        grid=(steps,),
        in_specs=[pl.BlockSpec((1, PACK_TM, D_MODEL),
                               lambda i: (layer, jnp.clip(i - 1, 0, N_EXPERTS // PACK_TM - 1), 0))],
        out_specs=pl.BlockSpec((TABLE_PAD, LANES), lambda i: (i, 0)),
        out_shape=jax.ShapeDtypeStruct((steps * TABLE_PAD, LANES), jnp.int32),
        compiler_params=pltpu.CompilerParams(dimension_semantics=("parallel",)),
        name="pack_table",
    )(tabs)


def _pack_kernel(tab_ref, out_ref):
    i = pl.program_id(0)
    interior = (i > 0) & (i < pl.num_programs(0) - 1)

    @pl.when(interior)
    def _():
        bits = _bf16_bits(tab_ref[0])
        words = bits[:, D_MODEL // 2:] | lax.shift_right_logical(bits[:, :D_MODEL // 2], 16)
        for s in range(HALF_ROWS):
            out_ref[pl.ds(s, PACK_TM, stride=HALF_ROWS), :] = words[:, s * LANES:(s + 1) * LANES]

    @pl.when(jnp.logical_not(interior))
    def _():
        out_ref[...] = jnp.zeros(out_ref.shape, jnp.int32)


def kernel(x, rel_bias, final_gain, attn_norm_gain, w_in, b_gate, attn_sinks, w_pool, pool_scale, gmlp_v_gain, w_spatial, b_spatial, w_branch_a, w_branch_b, w_branch_c, w_out, ffn_norm_gain, peer_w_query, peer_sub_keys1, peer_sub_keys2, peer_expert_u, peer_expert_v):
    B, S, D = x.shape
    T = B * S
    depth = w_in.shape[0]
    bf16 = jnp.bfloat16
    bias = _attention_bias(rel_bias)
    xt = x.reshape(T, D)
    for l in range(depth):
        q, kv, pz, guv, g = _in_proj(xt, attn_norm_gain[l][None], w_in[l].astype(bf16))
        xt = _mixer(xt, q, kv, pz, guv, g, attn_sinks[l], bias,
                    w_pool[l].astype(bf16), pool_scale[l][None], gmlp_v_gain[l][None],
                    w_spatial[l], b_spatial[l].T,
                    w_branch_a[l].astype(bf16), w_branch_b[l].astype(bf16),
                    w_branch_c[l].astype(bf16), w_out[l].astype(bf16), b_gate[l][None], B)
        h, expert, gate = _route(xt, ffn_norm_gain[l][None], peer_w_query[l].T.astype(bf16),
                                 peer_sub_keys1[l].astype(bf16), peer_sub_keys2[l].astype(bf16))
        expert = _by_slot(expert)
        coeff = _peer_u(expert, h, gate, _pack_table(peer_expert_u, l))
        xt = _peer_v(expert, _by_slot(coeff), xt.reshape(T, SUBLANES, LANES),
                     _pack_table(peer_expert_v, l)).reshape(T, D)
    return _final_norm(xt, final_gain[None]).reshape(B, S, D)
```
